```python
import jax, jax.numpy as jnp
from jax import lax
import numpy as np

D_MODEL = 1024
BATCH = 8
SEQ = 4096
DEPTH = 2

GRID_W = 64
CTX_LEN = 256

ATTN_HEADS = 8
ATTN_KV_HEADS = 2
Q_PER_KV = ATTN_HEADS // ATTN_KV_HEADS
HEAD_DIM = 64
ATTN_WIDTH = ATTN_HEADS * HEAD_DIM
KV_WIDTH = ATTN_KV_HEADS * HEAD_DIM
Q_BLOCK = 128
ROPE_THETA = 10000.0
ATTN_SCALE = HEAD_DIM ** -0.5

SSD_INNER = D_MODEL // 2
SSD_HEAD_DIM = 64
SSD_HEADS = SSD_INNER // SSD_HEAD_DIM
SSD_GROUPS = 2
SSD_STATE = 128
SSD_CONV = 5
SSD_CHUNK = 128
SSD_CONV_CH = SSD_INNER + 2 * SSD_GROUPS * SSD_STATE

CM_CH = D_MODEL // 2
CM_KERNEL = 31

N_BRANCH = 3
EPS = 1e-6

IN_SPLITS = (ATTN_WIDTH, KV_WIDTH, KV_WIDTH, ATTN_WIDTH,
             SSD_CONV_CH, 2 * SSD_HEADS, SSD_INNER,
             2 * CM_CH, CM_CH,
             N_BRANCH * D_MODEL)
IN_COLS = (2 * ATTN_WIDTH + 2 * KV_WIDTH + SSD_CONV_CH + 2 * SSD_HEADS + SSD_INNER
           + 3 * CM_CH + N_BRANCH * D_MODEL)

kernel_name = "hybrid_gated_attn_ssd_conformer_block"


def split_in(t):
    idx = []
    acc = 0
    for s in IN_SPLITS[:-1]:
        acc += s
        idx.append(acc)
    return jnp.split(t, idx, axis=-1)


def rmsnorm(x, w):
    xf = x.astype(jnp.float32)
    y = xf * lax.rsqrt(jnp.mean(xf * xf, axis=-1, keepdims=True) + EPS)
    return (y * w.astype(jnp.float32)).astype(x.dtype)


def layernorm(x, w, b):
    xf = x.astype(jnp.float32)
    mu = jnp.mean(xf, axis=-1, keepdims=True)
    d = xf - mu
    var = jnp.mean(d * d, axis=-1, keepdims=True)
    return (d * lax.rsqrt(var + EPS) * w.astype(jnp.float32) + b.astype(jnp.float32)).astype(x.dtype)


def depthwise_conv(x, w, b):
    k = w.shape[0]
    y = lax.conv_general_dilated(x, w[:, None, :].astype(x.dtype), window_strides=(1,),
                                 padding=[(k // 2, k // 2)],
                                 dimension_numbers=('NWC', 'WIO', 'NWC'),
                                 feature_group_count=x.shape[-1])
    return y + b


def axial_rope_tables(length):
    rows = length // GRID_W
    row = jnp.repeat(jnp.arange(rows), GRID_W).astype(jnp.float32)
    col = jnp.tile(jnp.arange(GRID_W), rows).astype(jnp.float32)
    n_freq = HEAD_DIM // 4
    inv = 1.0 / (ROPE_THETA ** (jnp.arange(n_freq, dtype=jnp.float32) / n_freq))
    ang_r = row[:, None] * inv
    ang_c = col[:, None] * inv
    return jnp.cos(ang_r), jnp.sin(ang_r), jnp.cos(ang_c), jnp.sin(ang_c)


def _rotate(x, cos, sin):
    x1, x2 = jnp.split(x, 2, axis=-1)
    return jnp.concatenate([x1 * cos - x2 * sin, x2 * cos + x1 * sin], axis=-1)


def apply_axial_rope(x, rope):
    cos_r, sin_r, cos_c, sin_c = rope
    xf = x.astype(jnp.float32)
    xr, xc = jnp.split(xf, 2, axis=-1)
    return jnp.concatenate([_rotate(xr, cos_r, sin_r), _rotate(xc, cos_c, sin_c)], axis=-1).astype(x.dtype)


def heads_q(t):
    b, l, _ = t.shape
    return t.reshape(b, l, ATTN_KV_HEADS, Q_PER_KV, HEAD_DIM).transpose(0, 2, 3, 1, 4)


def heads_kv(t):
    b, l, _ = t.shape
    return t.reshape(b, l, ATTN_KV_HEADS, HEAD_DIM).transpose(0, 2, 1, 3)


def merge_heads(o):
    b, kv, g, l, d = o.shape
    return o.transpose(0, 3, 1, 2, 4).reshape(b, l, kv * g * d)


def _attend(q, k, v):
    s = jnp.einsum('bkgqd,bksd->bkgqs', q, k).astype(jnp.float32) * ATTN_SCALE
    p = jax.nn.softmax(s, axis=-1).astype(v.dtype)
    return jnp.einsum('bkgqs,bksd->bkgqd', p, v)


def latent_attention(q, k, v, k_c, v_c):
    b, kv, g, l, d = q.shape
    nb = l // Q_BLOCK
    k_all = jnp.concatenate([k, k_c], axis=2)
    v_all = jnp.concatenate([v, v_c], axis=2)
    qb = q.reshape(b, kv, g, nb, Q_BLOCK, d).transpose(3, 0, 1, 2, 4, 5)
    ob = lax.map(lambda blk: _attend(blk, k_all, v_all), qb)
    return ob.transpose(1, 2, 3, 0, 4, 5).reshape(b, kv, g, l, d)


def segsum(x):
    t = x.shape[-1]
    xe = jnp.broadcast_to(x[..., :, None], x.shape + (t,))
    xs = jnp.cumsum(jnp.where(jnp.tril(jnp.ones((t, t), bool), -1), xe, 0.0), axis=-2)
    return jnp.where(jnp.tril(jnp.ones((t, t), bool)), xs, -jnp.inf)


def ssd_chunked(x, dt, a_head, bm, cm, init_state, with_output):
    b, l, h, p = x.shape
    n = bm.shape[-1]
    nc = l // SSD_CHUNK
    xd = (x * dt[..., None].astype(x.dtype)).reshape(b, nc, SSD_CHUNK, h, p)
    a = (dt * a_head).reshape(b, nc, SSD_CHUNK, h).transpose(0, 3, 1, 2)
    a_cum = jnp.cumsum(a, axis=-1)
    bc = bm.reshape(b, nc, SSD_CHUNK, h, n)
    cc = cm.reshape(b, nc, SSD_CHUNK, h, n)
    decay_states = jnp.exp(a_cum[..., -1:] - a_cum).astype(x.dtype)
    states = jnp.einsum('bclhn,bhcl,bclhp->bchpn', bc, decay_states, xd)
    states = jnp.concatenate([init_state[:, None].astype(x.dtype), states], axis=1)
    decay_chunk = jnp.exp(segsum(jnp.pad(a_cum[..., -1], ((0, 0), (0, 0), (1, 0))))).astype(x.dtype)
    new_states = jnp.einsum('bhzc,bchpn->bzhpn', decay_chunk, states)
    final_state = new_states[:, -1]
    if not with_output:
        return None, final_state
    states = new_states[:, :-1]
    lmat = jnp.exp(segsum(a)).astype(x.dtype)
    cb = jnp.einsum('bclhn,bcshn->bhcls', cc, bc)
    y_diag = jnp.einsum('bhcls,bcshp->bclhp', cb * lmat, xd)
    y_off = jnp.einsum('bclhn,bchpn,bhcl->bclhp', cc, states, jnp.exp(a_cum).astype(x.dtype))
    return (y_diag + y_off).reshape(b, l, h, p), final_state


def ssd_inputs(xbc, dt_raw, conv_w, conv_b, dt_bias):
    xbc = jax.nn.silu(depthwise_conv(xbc, conv_w, conv_b))
    b, l, _ = xbc.shape
    xs, bm, cm = jnp.split(xbc, [SSD_INNER, SSD_INNER + SSD_GROUPS * SSD_STATE], axis=-1)
    rep = SSD_HEADS // SSD_GROUPS
    xs = xs.reshape(b, l, SSD_HEADS, SSD_HEAD_DIM)
    bm = jnp.repeat(bm.reshape(b, l, SSD_GROUPS, SSD_STATE), rep, axis=2)
    cm = jnp.repeat(cm.reshape(b, l, SSD_GROUPS, SSD_STATE), rep, axis=2)
    dt = jax.nn.softplus(dt_raw.astype(jnp.float32).reshape(b, l, 2, SSD_HEADS)
                         + dt_bias.astype(jnp.float32))
    return xs, bm, cm, dt[:, :, 0], dt[:, :, 1]


def _flip(t):
    return jnp.flip(t, axis=1)


def bidir_ssd(xs, bm, cm, dt_f, dt_b, a, init_f, init_b, d_skip, with_output):
    y_f, s_f = ssd_chunked(xs, dt_f, a[0], bm, cm, init_f, with_output)
    y_b, s_b = ssd_chunked(_flip(xs), _flip(dt_b), a[1], _flip(bm), _flip(cm), init_b, with_output)
    if not with_output:
        return None, s_f, s_b
    y = y_f + _flip(y_b) + d_skip[:, None].astype(xs.dtype) * xs
    return y.reshape(xs.shape[0], xs.shape[1], SSD_INNER), s_f, s_b


def conformer_conv(glu, gate, conv_w, conv_b, ln_w, ln_b):
    u, g = jnp.split(glu, 2, axis=-1)
    v = u * jax.nn.sigmoid(g)
    v = depthwise_conv(v, conv_w, conv_b)
    v = jax.nn.silu(layernorm(v, ln_w, ln_b))
    return v * jax.nn.silu(gate)


def gated_merge(ua, us, uc, gm, w_br_attn, w_br_ssd, w_br_conv, b_gate, w_out):
    g = jax.nn.sigmoid(gm.reshape(gm.shape[:-1] + (N_BRANCH, D_MODEL)) + b_gate)
    y = (g[..., 0, :] * (ua @ w_br_attn) + g[..., 1, :] * (us @ w_br_ssd)
         + g[..., 2, :] * (uc @ w_br_conv))
    return y @ w_out


def hybrid_layer(x, xc, c, c_ctx, w_mod, b_mod, norm_w, w_in, q_norm_w, k_norm_w,
                 ssd_conv_w, ssd_conv_b, ssd_A_log, ssd_dt_bias, ssd_D, ssd_norm_w,
                 cm_conv_w, cm_conv_b, cm_ln_w, cm_ln_b,
                 w_br_attn, w_br_ssd, w_br_conv, b_gate, w_out, rope, last):
    b = x.shape[0]
    mod = jax.nn.silu(c) @ w_mod + b_mod
    mod_c = jax.nn.silu(c_ctx) @ w_mod + b_mod
    shift, scale, gate = jnp.split(mod[:, None, :], 3, axis=-1)
    shift_c, scale_c, gate_c = jnp.split(mod_c, 3, axis=-1)
    h = rmsnorm(x, norm_w) * (1.0 + scale) + shift
    hc = rmsnorm(xc, norm_w) * (1.0 + scale_c) + shift_c

    q, k, v, ga, xbc, dt, z, glu, gcv, gm = split_in(h @ w_in)
    if last:
        w_parts = split_in(w_in)
        k_c, v_c, xbc_c, dt_c = (hc @ w_parts[1], hc @ w_parts[2], hc @ w_parts[4], hc @ w_parts[5])
    else:
        q_c, k_c, v_c, ga_c, xbc_c, dt_c, z_c, glu_c, gcv_c, gm_c = split_in(hc @ w_in)

    qh = apply_axial_rope(rmsnorm(heads_q(q), q_norm_w), rope)
    kh = apply_axial_rope(rmsnorm(heads_kv(k), k_norm_w), rope)
    kh_c = rmsnorm(heads_kv(k_c), k_norm_w)
    vh_c = heads_kv(v_c)
    ua = merge_heads(latent_attention(qh, kh, heads_kv(v), kh_c, vh_c)) * jax.nn.silu(ga)

    a = -jnp.exp(ssd_A_log.astype(jnp.float32))
    xs_c, bm_c, cm_c, dtf_c, dtb_c = ssd_inputs(xbc_c, dt_c, ssd_conv_w, ssd_conv_b, ssd_dt_bias)
    zero = jnp.zeros((b, SSD_HEADS, SSD_HEAD_DIM, SSD_STATE), xs_c.dtype)
    yc, s_f, s_b = bidir_ssd(xs_c, bm_c, cm_c, dtf_c, dtb_c, a, zero, zero, ssd_D, not last)
    xs, bm, cm, dtf, dtb = ssd_inputs(xbc, dt, ssd_conv_w, ssd_conv_b, ssd_dt_bias)
    y, _, _ = bidir_ssd(xs, bm, cm, dtf, dtb, a, s_f, s_b, ssd_D, True)
    us = rmsnorm(y * jax.nn.silu(z), ssd_norm_w)

    uc = conformer_conv(glu, gcv, cm_conv_w, cm_conv_b, cm_ln_w, cm_ln_b)

    out = gated_merge(ua, us, uc, gm, w_br_attn, w_br_ssd, w_br_conv, b_gate, w_out)
    x_new = x + gate * out
    if last:
        return x_new, None

    qh_c = rmsnorm(heads_q(q_c), q_norm_w)
    ua_c = merge_heads(_attend(qh_c, kh_c, vh_c)) * jax.nn.silu(ga_c)
    us_c = rmsnorm(yc * jax.nn.silu(z_c), ssd_norm_w)
    uc_c = conformer_conv(glu_c, gcv_c, cm_conv_w, cm_conv_b, cm_ln_w, cm_ln_b)
    out_c = gated_merge(ua_c, us_c, uc_c, gm_c, w_br_attn, w_br_ssd, w_br_conv, b_gate, w_out)
    return x_new, xc + gate_c * out_c


def setup_inputs(seed: int = 0) -> dict:
    key = jax.random.key(seed)
    ks = jax.random.split(key, 32)
    f32 = jnp.float32

    def nrm(k, shape, s):
        return jax.random.normal(k, shape, f32) * s

    def gain(k, shape):
        return 1.0 + 0.05 * jax.random.normal(k, shape, f32)

    dt0 = jnp.exp(jax.random.uniform(ks[10], (DEPTH, 2, SSD_HEADS), f32,
                                     math_log(0.001), math_log(0.1)))
    return {
        'x': nrm(ks[0], (BATCH, SEQ, D_MODEL), 1.0),
        'c': nrm(ks[1], (BATCH, D_MODEL), 1.0),
        'ctx': nrm(ks[2], (BATCH, CTX_LEN, D_MODEL), 1.0),
        'c_ctx': nrm(ks[3], (D_MODEL,), 1.0),
        'w_mod': nrm(ks[4], (DEPTH, D_MODEL, 3 * D_MODEL), 0.5 * D_MODEL ** -0.5),
        'b_mod': nrm(ks[5], (DEPTH, 3 * D_MODEL), 0.01),
        'norm_w': gain(ks[6], (DEPTH, D_MODEL)),
        'w_in': nrm(ks[7], (DEPTH, D_MODEL, IN_COLS), D_MODEL ** -0.5),
        'q_norm_w': gain(ks[8], (DEPTH, HEAD_DIM)),
        'k_norm_w': gain(ks[9], (DEPTH, HEAD_DIM)),
        'ssd_conv_w': nrm(ks[11], (DEPTH, SSD_CONV, SSD_CONV_CH), SSD_CONV ** -0.5),
        'ssd_conv_b': nrm(ks[12], (DEPTH, SSD_CONV_CH), 0.02),
        'ssd_A_log': jnp.log(jax.random.uniform(ks[13], (DEPTH, 2, SSD_HEADS), f32, 1.0, 16.0)),
        'ssd_dt_bias': dt0 + jnp.log(-jnp.expm1(-dt0)),
        'ssd_D': gain(ks[14], (DEPTH, SSD_HEADS)),
        'ssd_norm_w': gain(ks[15], (DEPTH, SSD_INNER)),
        'cm_conv_w': nrm(ks[16], (DEPTH, CM_KERNEL, CM_CH), CM_KERNEL ** -0.5),
        'cm_conv_b': nrm(ks[17], (DEPTH, CM_CH), 0.02),
        'cm_ln_w': gain(ks[18], (DEPTH, CM_CH)),
        'cm_ln_b': nrm(ks[19], (DEPTH, CM_CH), 0.02),
        'w_br_attn': nrm(ks[20], (DEPTH, ATTN_WIDTH, D_MODEL), ATTN_WIDTH ** -0.5),
        'w_br_ssd': nrm(ks[21], (DEPTH, SSD_INNER, D_MODEL), SSD_INNER ** -0.5),
        'w_br_conv': nrm(ks[22], (DEPTH, CM_CH, D_MODEL), CM_CH ** -0.5),
        'b_gate': nrm(ks[23], (DEPTH, N_BRANCH, D_MODEL), 0.1),
        'w_out': nrm(ks[24], (DEPTH, D_MODEL, D_MODEL), D_MODEL ** -0.5),
        'final_norm_w': gain(ks[25], (D_MODEL,)),
    }


def math_log(v):
    return float(np.log(v))


def reference(x, c, ctx, c_ctx, w_mod, b_mod, norm_w, w_in, q_norm_w, k_norm_w,
              ssd_conv_w, ssd_conv_b, ssd_A_log, ssd_dt_bias, ssd_D, ssd_norm_w,
              cm_conv_w, cm_conv_b, cm_ln_w, cm_ln_b,
              w_br_attn, w_br_ssd, w_br_conv, b_gate, w_out, final_norm_w):
    rope = axial_rope_tables(x.shape[1])
    xc = ctx
    for i in range(DEPTH):
        x, xc = hybrid_layer(x, xc, c, c_ctx, w_mod[i], b_mod[i], norm_w[i], w_in[i],
                             q_norm_w[i], k_norm_w[i], ssd_conv_w[i], ssd_conv_b[i],
                             ssd_A_log[i], ssd_dt_bias[i], ssd_D[i], ssd_norm_w[i],
                             cm_conv_w[i], cm_conv_b[i], cm_ln_w[i], cm_ln_b[i],
                             w_br_attn[i], w_br_ssd[i], w_br_conv[i], b_gate[i], w_out[i],
                             rope, i == DEPTH - 1)
    return rmsnorm(x, final_norm_w)
```

```python
import functools

import numpy as np
import jax
import jax.numpy as jnp
from jax import lax
from jax.experimental import pallas as pl
from jax.experimental.pallas import tpu as pltpu

F32 = jnp.float32
BF16 = jnp.bfloat16

GRID_W = 64
HEAD_DIM = 64
ATTN_HEADS = 8
ATTN_KV_HEADS = 2
ATTN_WIDTH = ATTN_HEADS * HEAD_DIM
KV_WIDTH = ATTN_KV_HEADS * HEAD_DIM
ROPE_THETA = 10000.0
ATTN_SCALE = HEAD_DIM ** -0.5
SSD_HEADS = 8
SSD_HEAD_DIM = 64
SSD_INNER = SSD_HEADS * SSD_HEAD_DIM
SSD_GROUPS = 2
SSD_STATE = 128
SSD_CONV = 5
SSD_CHUNK = 128
CM_CH = 512
CM_KERNEL = 31
N_BRANCH = 3
EPS = 1e-6

LANES = 128
SUBLANES = 8
ROW_TILE = 512
Q_TILE = 512


def _dot(a, b):
    return jnp.dot(a, b, preferred_element_type=F32)


def _dot_nt(a, b):
    return lax.dot_general(a, b, (((1,), (1,)), ((), ())), preferred_element_type=F32)


def _split3(x):
    x1 = x.astype(BF16)
    r1 = x - x1.astype(F32)
    x2 = r1.astype(BF16)
    r2 = r1 - x2.astype(F32)
    return x1, x2, r2.astype(BF16)


def _dot_exact_lhs(x, m):
    x1, x2, x3 = _split3(x)
    return _dot(x1, m) + _dot(x2, m) + _dot(x3, m)


def _dot_exact_rhs(m, x):
    x1, x2, x3 = _split3(x)
    return _dot(m, x1) + _dot(m, x2) + _dot(m, x3)


def _sigmoid(x):
    return 1.0 / (1.0 + jnp.exp(-x))


def _silu(x):
    return x * _sigmoid(x)


def _softplus(x):
    return jnp.maximum(x, 0.0) + jnp.log1p(jnp.exp(-jnp.abs(x)))


def _modulated_norm(x, nw, scale, shift):
    ms = jnp.mean(x * x, axis=-1, keepdims=True)
    return x * lax.rsqrt(ms + EPS) * (nw * (1.0 + scale)) + shift


def _rms(x, w):
    ms = jnp.mean(x * x, axis=-1, keepdims=True)
    return x * lax.rsqrt(ms + EPS) * w


def _mod_kernel(c_ref, w_ref, b_ref, o_ref):
    a1, a2, a3 = _split3(_silu(c_ref[...]))
    w1, w2, w3 = _split3(w_ref[0])
    o = (_dot(a1, w1) + _dot(a1, w2) + _dot(a2, w1)
         + _dot(a2, w2) + _dot(a1, w3) + _dot(a3, w1))
    o_ref[0] = o + b_ref[0]


def _modulation(c_rows, w_mod, b_mod):
    depth, d, d3 = w_mod.shape
    rows = c_rows.shape[0]
    col_tile = 768
    return pl.pallas_call(
        _mod_kernel,
        out_shape=jax.ShapeDtypeStruct((depth, rows, d3), F32),
        grid=(depth, d3 // col_tile),
        in_specs=[
            pl.BlockSpec((rows, d), lambda i, j: (0, 0)),
            pl.BlockSpec((1, d, col_tile), lambda i, j: (i, 0, j)),
            pl.BlockSpec((1, 1, col_tile), lambda i, j: (i, 0, j)),
        ],
        out_specs=pl.BlockSpec((1, rows, col_tile), lambda i, j: (i, 0, j)),
        name="modulation",
    )(c_rows, w_mod, b_mod.reshape(depth, 1, d3))


def _head_norm(t, gmat, w):
    s = t * t
    s1 = s.astype(BF16)
    s2 = (s - s1.astype(F32)).astype(BF16)
    ms = _dot(s1, gmat) + _dot(s2, gmat)
    return t * lax.rsqrt(ms + EPS) * w


def _rope(t, cos, sin_lo, sin_hi):
    outs = []
    for j in range(t.shape[1] // LANES):
        tj = t[:, LANES * j:LANES * (j + 1)]
        outs.append(tj * cos + pltpu.roll(tj, LANES - 16, 1) * sin_lo
                    + pltpu.roll(tj, 16, 1) * sin_hi)
    return outs[0] if len(outs) == 1 else jnp.concatenate(outs, axis=1)


def _attn_in_kernel(*refs, rope, need_q):
    if rope:
        (x_ref, sc_ref, sh_ref, nw_ref, w_ref, g_ref, qnw_ref, knw_ref,
         cos_ref, slo_ref, shi_ref, *outs) = refs
    else:
        (x_ref, sc_ref, sh_ref, nw_ref, w_ref, g_ref, qnw_ref, knw_ref, *outs) = refs
    h = _modulated_norm(x_ref[0], nw_ref[...], sc_ref[0], sh_ref[0]).astype(BF16)
    p = _dot(h, w_ref[...])
    k = _head_norm(p[:, :KV_WIDTH], g_ref[:KV_WIDTH, :KV_WIDTH], knw_ref[...])
    if rope:
        k = _rope(k, cos_ref[...], slo_ref[...], shi_ref[...])
    if need_q:
        k_ref, v_ref, q_ref, ga_ref = outs
    else:
        k_ref, v_ref = outs
    k_ref[0] = k.astype(BF16)
    v_ref[0] = p[:, KV_WIDTH:2 * KV_WIDTH].astype(BF16)
    if need_q:
        q0 = 2 * KV_WIDTH
        q = _head_norm(p[:, q0:q0 + ATTN_WIDTH], g_ref[...], qnw_ref[...])
        if rope:
            q = _rope(q, cos_ref[...], slo_ref[...], shi_ref[...])
        q_ref[0] = (q * ATTN_SCALE).astype(BF16)
        ga_ref[0] = _silu(p[:, q0 + ATTN_WIDTH:])


def _attn_in(x, scale, shift, nw, w, gmat, qnw, knw, rope_tabs, need_q):
    b, l, d = x.shape
    tm = min(ROW_TILE, l)
    rope = rope_tabs is not None
    cols = w.shape[1]
    vec = lambda n: pl.BlockSpec((1, n), lambda bi, i: (0, 0))
    in_specs = [
        pl.BlockSpec((1, tm, d), lambda bi, i: (bi, i, 0)),
        pl.BlockSpec((1, 1, d), lambda bi, i: (bi, 0, 0)),
        pl.BlockSpec((1, 1, d), lambda bi, i: (bi, 0, 0)),
        vec(d),
        pl.BlockSpec((d, cols), lambda bi, i: (0, 0)),
        pl.BlockSpec((ATTN_WIDTH, ATTN_WIDTH), lambda bi, i: (0, 0)),
        vec(ATTN_WIDTH),
        vec(KV_WIDTH),
    ]
    args = [x, scale, shift, nw, w, gmat, qnw, knw]
    if rope:
        in_specs += [pl.BlockSpec((tm, LANES), lambda bi, i: (i, 0))] * 3
        args += list(rope_tabs)
    tok = lambda n: pl.BlockSpec((1, tm, n), lambda bi, i: (bi, i, 0))
    out_shape = [jax.ShapeDtypeStruct((b, l, KV_WIDTH), BF16)] * 2
    out_specs = [tok(KV_WIDTH), tok(KV_WIDTH)]
    if need_q:
        out_shape += [jax.ShapeDtypeStruct((b, l, ATTN_WIDTH), BF16),
                      jax.ShapeDtypeStruct((b, l, ATTN_WIDTH), F32)]
        out_specs += [tok(ATTN_WIDTH), tok(ATTN_WIDTH)]
    return pl.pallas_call(
        functools.partial(_attn_in_kernel, rope=rope, need_q=need_q),
        out_shape=out_shape,
        grid=(b, l // tm),
        in_specs=in_specs,
        out_specs=out_specs,
        name="attn_in",
    )(*args)


def _flash_kernel(q_ref, k_ref, v_ref, ga_ref, o_ref):
    group = pl.program_id(2) // 2
    k = k_ref[0]
    v = v_ref[0]
    q2 = q_ref[0].astype(F32)
    q2_swapped = pltpu.roll(q2, HEAD_DIM, 1)
    lane_half = lax.broadcasted_iota(jnp.int32, (1, LANES), 1) // HEAD_DIM
    in_group = lane_half == group
    heads = []
    for hh in range(2):
        src = jnp.where(group == hh, q2, q2_swapped)
        qp = jnp.where(in_group, src, 0.0).astype(BF16)
        s = _dot_nt(qp, k)
        p = jnp.exp(s - jnp.max(s, axis=-1, keepdims=True))
        denom = jnp.sum(p, axis=-1, keepdims=True)
        o = _dot(p.astype(BF16), v) / denom
        heads.append(jnp.where(group == hh, o, pltpu.roll(o, HEAD_DIM, 1)))
    o2 = jnp.where(lane_half == 0, heads[0], heads[1])
    o_ref[0] = o2 * ga_ref[0]


def _flash(q, k, v, ga):
    b, l, _ = q.shape
    s = k.shape[1]
    tq = min(Q_TILE, l)
    return pl.pallas_call(
        _flash_kernel,
        out_shape=jax.ShapeDtypeStruct((b, l, ATTN_WIDTH), F32),
        grid=(b, l // tq, ATTN_WIDTH // LANES),
        in_specs=[
            pl.BlockSpec((1, tq, LANES), lambda bi, i, m: (bi, i, m)),
            pl.BlockSpec((1, s, KV_WIDTH), lambda bi, i, m: (bi, 0, 0)),
            pl.BlockSpec((1, s, KV_WIDTH), lambda bi, i, m: (bi, 0, 0)),
            pl.BlockSpec((1, tq, LANES), lambda bi, i, m: (bi, i, m)),
        ],
        out_specs=pl.BlockSpec((1, tq, LANES), lambda bi, i, m: (bi, i, m)),
        name="flash",
    )(q, k, v, ga)


def _halo_rows_valid(rows, halo, tile, i, n_tiles):
    r = lax.broadcasted_iota(jnp.int32, (rows, 1), 0)
    head_ok = jnp.where(i > 0, 0, halo)
    tail_ok = jnp.where(i < n_tiles - 1, rows, halo + tile)
    return (r >= head_ok) & (r < tail_ok)


def _ssd_in_kernel(xp_ref, x_ref, xn_ref, sc_ref, sh_ref, nw_ref, w_ref, cw_ref, cb_ref,
                   dtb_ref, xbc_ref, dt_ref, z_ref, *, n_tiles):
    i = pl.program_id(1)
    t = x_ref.shape[1]
    halo = SUBLANES
    xe = jnp.concatenate([xp_ref[0], x_ref[0], xn_ref[0]], axis=0)
    h = _modulated_norm(xe, nw_ref[...], sc_ref[0], sh_ref[0]).astype(BF16)
    conv_ch = xbc_ref.shape[2]
    raw = _dot(h, w_ref[:, :conv_ch])
    raw = jnp.where(_halo_rows_valid(t + 2 * halo, halo, t, i, n_tiles), raw, 0.0)
    acc = jnp.broadcast_to(cb_ref[...], (t, conv_ch))
    for k in range(SSD_CONV):
        off = halo - SSD_CONV // 2 + k
        acc = acc + cw_ref[k:k + 1, :] * raw[off:off + t, :]
    xbc_ref[0] = _silu(acc)
    rest = _dot(h[halo:halo + t], w_ref[:, conv_ch:])
    dt_ref[0] = _softplus(rest[:, :2 * LANES] + dtb_ref[...])
    z_ref[0] = rest[:, 2 * LANES:]


def _halo_specs(tm, d, halo, l):
    per = tm // halo
    last = l // halo - 1
    prev = pl.BlockSpec((1, halo, d), lambda bi, i: (bi, jnp.maximum(i * per - 1, 0), 0))
    cur = pl.BlockSpec((1, tm, d), lambda bi, i: (bi, i, 0))
    nxt = pl.BlockSpec((1, halo, d), lambda bi, i: (bi, jnp.minimum((i + 1) * per, last), 0))
    return [prev, cur, nxt]


def _ssd_in(x, scale, shift, nw, w, conv_w, conv_b, dt_bias):
    b, l, d = x.shape
    tm = min(ROW_TILE, l)
    n_tiles = l // tm
    cols = w.shape[1]
    conv_ch = conv_w.shape[1]
    const = lambda shp: pl.BlockSpec(shp, lambda bi, i: (0,) * len(shp))
    tok = lambda n: pl.BlockSpec((1, tm, n), lambda bi, i: (bi, i, 0))
    return pl.pallas_call(
        functools.partial(_ssd_in_kernel, n_tiles=n_tiles),
        out_shape=[jax.ShapeDtypeStruct((b, l, conv_ch), F32),
                   jax.ShapeDtypeStruct((b, l, 2 * LANES), F32),
                   jax.ShapeDtypeStruct((b, l, SSD_INNER), F32)],
        grid=(b, n_tiles),
        in_specs=_halo_specs(tm, d, SUBLANES, l) + [
            pl.BlockSpec((1, 1, d), lambda bi, i: (bi, 0, 0)),
            pl.BlockSpec((1, 1, d), lambda bi, i: (bi, 0, 0)),
            const((1, d)), const((d, cols)), const(conv_w.shape), const((1, conv_ch)),
            const((1, 2 * LANES)),
        ],
        out_specs=[tok(conv_ch), tok(2 * LANES), tok(SSD_INNER)],
        name="ssd_in",
    )(x, x, x, scale, shift, nw, w, conv_w, conv_b, dt_bias)


def _ssd_scan_kernel(xs_ref, bm_ref, cm_ref, dt_ref, a_ref, tri_ref, mask_ref, e_ref,
                     init_ref, y_ref, fin_ref, st_ref):
    c = pl.program_id(2)
    q = xs_ref.shape[1]

    @pl.when(c == 0)
    def _():
        st_ref[...] = init_ref[0, 0]

    xs = xs_ref[0]
    dt = dt_ref[0]
    a = dt * a_ref[0]
    cum = _dot_exact_rhs(tri_ref[0], a)
    acum = cum[:q]
    stacked = jnp.concatenate([cum, dt], axis=0)
    wide = _dot_exact_lhs(stacked, e_ref[...])
    acum_w = wide[:q]
    tot_w = wide[q:2 * q]
    xd = xs * wide[2 * q:]
    decay_out = jnp.exp(acum_w)
    xdd = (xd * jnp.exp(tot_w - acum_w)).astype(BF16)
    xdb = xd.astype(BF16)
    st = st_ref[...]
    stb = st.astype(BF16)
    acum_t = acum.T
    mask = mask_ref[0] > 0.5
    lane_lo = lax.broadcasted_iota(jnp.int32, (1, LANES), 1) < SSD_HEAD_DIM
    gw = SSD_INNER // SSD_GROUPS
    y_diag, y_off, upd = [], [], []
    for g in range(SSD_GROUPS):
        bg = bm_ref[0, :, SSD_STATE * g:SSD_STATE * (g + 1)]
        cg = cm_ref[0, :, SSD_STATE * g:SSD_STATE * (g + 1)].astype(BF16)
        bg_t = bg.T.astype(BF16)
        cb = _dot(cg, bg_t)
        y_off.append(_dot(cg, stb[:, gw * g:gw * (g + 1)]))
        upd.append(_dot(bg_t, xdd[:, gw * g:gw * (g + 1)]))
        for pr in range(gw // LANES):
            blk = (gw // LANES) * g + pr
            xpair = xdb[:, LANES * blk:LANES * (blk + 1)]
            res = []
            for hh in range(2):
                head = 2 * blk + hh
                seg = acum[:, head:head + 1] - acum_t[head:head + 1, :]
                lmat = jnp.exp(jnp.where(mask, seg, -jnp.inf))
                res.append(_dot((cb * lmat).astype(BF16), xpair))
            y_diag.append(jnp.where(lane_lo, res[0], res[1]))
    y = jnp.concatenate(y_diag, axis=1) + jnp.concatenate(y_off, axis=1) * decay_out
    y_ref[0, 0] = y
    st_new = jnp.exp(tot_w[0:1]) * st + jnp.concatenate(upd, axis=1)
    st_ref[...] = st_new
    fin_ref[0, 0] = st_new


def _scan_constants(q):
    r = np.arange(q)
    lower = (r[:, None] >= r[None, :]).astype(np.float32)
    ones = np.ones((q, q), np.float32)
    tri = np.stack([np.concatenate([lower, ones], 0), np.concatenate([lower.T, ones], 0)])
    mask = np.stack([lower, lower.T])
    expand = np.zeros((LANES, SSD_INNER), np.float32)
    for hd in range(SSD_HEADS):
        expand[hd, SSD_HEAD_DIM * hd:SSD_HEAD_DIM * (hd + 1)] = 1.0
    return jnp.asarray(tri, BF16), jnp.asarray(mask, F32), jnp.asarray(expand, BF16)


def _ssd_scan(xbc, dt, a_vec, init):
    b, l, _ = xbc.shape
    q = SSD_CHUNK
    nc = l // q
    tri, mask, expand = _scan_constants(q)
    chunk = lambda bi, d, c: c + d * (nc - 1 - 2 * c)
    xs_w = SSD_INNER // LANES
    bc_w = SSD_GROUPS * SSD_STATE
    return pl.pallas_call(
        _ssd_scan_kernel,
        out_shape=[jax.ShapeDtypeStruct((2, b, l, SSD_INNER), F32),
                   jax.ShapeDtypeStruct((b, 2, SSD_STATE, SSD_INNER), F32)],
        grid=(b, 2, nc),
        in_specs=[
            pl.BlockSpec((1, q, SSD_INNER), lambda bi, d, c: (bi, chunk(bi, d, c), 0)),
            pl.BlockSpec((1, q, bc_w), lambda bi, d, c: (bi, chunk(bi, d, c), SSD_INNER // bc_w)),
            pl.BlockSpec((1, q, bc_w), lambda bi, d, c: (bi, chunk(bi, d, c), SSD_INNER // bc_w + 1)),
            pl.BlockSpec((1, q, LANES), lambda bi, d, c: (bi, chunk(bi, d, c), d)),
            pl.BlockSpec((1, 1, LANES), lambda bi, d, c: (d, 0, 0)),
            pl.BlockSpec((1, 2 * q, q), lambda bi, d, c: (d, 0, 0)),
            pl.BlockSpec((1, q, q), lambda bi, d, c: (d, 0, 0)),
            pl.BlockSpec((LANES, SSD_INNER), lambda bi, d, c: (0, 0)),
            pl.BlockSpec((1, 1, SSD_STATE, SSD_INNER), lambda bi, d, c: (bi, d, 0, 0)),
        ],
        out_specs=[
            pl.BlockSpec((1, 1, q, SSD_INNER), lambda bi, d, c: (d, bi, chunk(bi, d, c), 0)),
            pl.BlockSpec((1, 1, SSD_STATE, SSD_INNER), lambda bi, d, c: (bi, d, 0, 0)),
        ],
        scratch_shapes=[pltpu.VMEM((SSD_STATE, SSD_INNER), F32)],
        name="ssd_scan",
    )(xbc, xbc, xbc, dt, a_vec, tri, mask, expand, init)


def _conv_kernel(xp_ref, x_ref, xn_ref, sc_ref, sh_ref, nw_ref, w_ref, cw_ref, cb_ref,
                 lnw_ref, lnb_ref, o_ref, *, n_tiles):
    i = pl.program_id(1)
    t = x_ref.shape[1]
    halo = 2 * SUBLANES
    ch = o_ref.shape[2]
    xe = jnp.concatenate([xp_ref[0], x_ref[0], xn_ref[0]], axis=0)
    h = _modulated_norm(xe, nw_ref[...], sc_ref[0], sh_ref[0]).astype(BF16)
    ug = _dot(h, w_ref[:, :2 * ch])
    v = ug[:, :ch] * _sigmoid(ug[:, ch:])
    v = jnp.where(_halo_rows_valid(t + 2 * halo, halo, t, i, n_tiles), v, 0.0)
    acc = jnp.broadcast_to(cb_ref[...], (t, ch))
    first = halo - CM_KERNEL // 2
    for r in range(SUBLANES):
        vr = v[r:r + t + halo + SUBLANES]
        for al in range(0, halo + SUBLANES + 1, SUBLANES):
            k = al + r - first
            if 0 <= k < CM_KERNEL:
                acc = acc + cw_ref[k:k + 1, :] * vr[al:al + t]
    mu = jnp.mean(acc, axis=-1, keepdims=True)
    dev = acc - mu
    var = jnp.mean(dev * dev, axis=-1, keepdims=True)
    y = _silu(dev * lax.rsqrt(var + EPS) * lnw_ref[...] + lnb_ref[...])
    gate = _dot(h[halo:halo + t], w_ref[:, 2 * ch:])
    o_ref[0] = y * _silu(gate)


def _conformer(x, scale, shift, nw, w, conv_w, conv_b, ln_w, ln_b):
    b, l, d = x.shape
    tm = min(ROW_TILE, l)
    n_tiles = l // tm
    ch = conv_w.shape[1]
    const = lambda shp: pl.BlockSpec(shp, lambda bi, i: (0,) * len(shp))
    return pl.pallas_call(
        functools.partial(_conv_kernel, n_tiles=n_tiles),
        out_shape=jax.ShapeDtypeStruct((b, l, ch), F32),
        grid=(b, n_tiles),
        in_specs=_halo_specs(tm, d, 2 * SUBLANES, l) + [
            pl.BlockSpec((1, 1, d), lambda bi, i: (bi, 0, 0)),
            pl.BlockSpec((1, 1, d), lambda bi, i: (bi, 0, 0)),
            const((1, d)), const(w.shape), const(conv_w.shape), const((1, ch)),
            const((1, ch)), const((1, ch)),
        ],
        out_specs=pl.BlockSpec((1, tm, ch), lambda bi, i: (bi, i, 0)),
        name="conformer",
    )(x, x, x, scale, shift, nw, w, conv_w, conv_b, ln_w, ln_b)


def _merge_kernel(x_ref, sc_ref, sh_ref, gt_ref, nw_ref, ua_ref, yf_ref, yb_ref, xs_ref, z_ref,
                  uc_ref, wgm_ref, bg_ref, wa_ref, ws_ref, wc_ref, wo_ref, dsk_ref, snw_ref,
                  fnw_ref, o_ref, *, last):
    x = x_ref[0]
    d = x.shape[1]
    h = _modulated_norm(x, nw_ref[...], sc_ref[0], sh_ref[0]).astype(BF16)
    y = yf_ref[0, 0] + yb_ref[0, 0] + dsk_ref[...] * xs_ref[0]
    us = _rms(y * _silu(z_ref[0]), snw_ref[...])
    acc = None
    for j, (u, w_ref) in enumerate(((ua_ref[0], wa_ref), (us, ws_ref), (uc_ref[0], wc_ref))):
        gate = _sigmoid(_dot(h, wgm_ref[:, d * j:d * (j + 1)]) + bg_ref[j:j + 1, :])
        term = gate * _dot(u.astype(BF16), w_ref[...])
        acc = term if acc is None else acc + term
    out = _dot(acc.astype(BF16), wo_ref[...])
    xn = x + gt_ref[0] * out
    if last:
        xn = _rms(xn, fnw_ref[...])
    o_ref[0] = xn


def _merge(x, scale, shift, gate, nw, ua, y2, xbc, z, uc, wgm, bgate, wa, ws, wc, wo,
           dskip, snw, fnw, last):
    b, l, d = x.shape
    tm = min(ROW_TILE // 2, l)
    const = lambda shp: pl.BlockSpec(shp, lambda bi, i: (0,) * len(shp))
    tok = lambda n: pl.BlockSpec((1, tm, n), lambda bi, i: (bi, i, 0))
    per_b = pl.BlockSpec((1, 1, d), lambda bi, i: (bi, 0, 0))
    return pl.pallas_call(
        functools.partial(_merge_kernel, last=last),
        out_shape=jax.ShapeDtypeStruct((b, l, d), F32),
        grid=(b, l // tm),
        in_specs=[
            tok(d), per_b, per_b, per_b, const((1, d)),
            tok(ATTN_WIDTH),
            pl.BlockSpec((1, 1, tm, SSD_INNER), lambda bi, i: (0, bi, i, 0)),
            pl.BlockSpec((1, 1, tm, SSD_INNER), lambda bi, i: (1, bi, i, 0)),
            tok(SSD_INNER), tok(SSD_INNER), tok(CM_CH),
            const(wgm.shape), const(bgate.shape), const(wa.shape), const(ws.shape),
            const(wc.shape), const(wo.shape), const((1, SSD_INNER)), const((1, SSD_INNER)),
            const((1, d)),
        ],
        out_specs=tok(d),
        name="merge",
    )(x, scale, shift, gate, nw, ua, y2, y2, xbc, z, uc, wgm, bgate, wa, ws, wc, wo,
      dskip, snw, fnw)


def _rope_tables(length):
    rows = length // GRID_W
    row = jnp.repeat(jnp.arange(rows), GRID_W).astype(F32)
    col = jnp.tile(jnp.arange(GRID_W), rows).astype(F32)
    n_freq = HEAD_DIM // 4
    inv = 1.0 / (ROPE_THETA ** (jnp.arange(n_freq, dtype=F32) / n_freq))
    ang_r = row[:, None] * inv
    ang_c = col[:, None] * inv
    zero = jnp.zeros_like(ang_r)
    cos_head = jnp.concatenate([jnp.cos(ang_r)] * 2 + [jnp.cos(ang_c)] * 2, axis=1)
    lo_head = jnp.concatenate([-jnp.sin(ang_r), zero, -jnp.sin(ang_c), zero], axis=1)
    hi_head = jnp.concatenate([zero, jnp.sin(ang_r), zero, jnp.sin(ang_c)], axis=1)
    two = lambda t: jnp.concatenate([t, t], axis=1)
    return two(cos_head), two(lo_head), two(hi_head)


def _head_mean_matrix():
    idx = np.arange(ATTN_WIDTH) // HEAD_DIM
    return jnp.asarray((idx[:, None] == idx[None, :]).astype(np.float32) / HEAD_DIM, BF16)


def _pad_cols(a, width):
    return jnp.pad(a, ((0, 0), (0, width - a.shape[1])))


def kernel(x, c, ctx, c_ctx, w_mod, b_mod, norm_w, w_in, q_norm_w, k_norm_w, ssd_conv_w,
           ssd_conv_b, ssd_A_log, ssd_dt_bias, ssd_D, ssd_norm_w, cm_conv_w, cm_conv_b,
           cm_ln_w, cm_ln_b, w_br_attn, w_br_ssd, w_br_conv, b_gate, w_out, final_norm_w):
    b, l, d = x.shape
    depth = w_mod.shape[0]
    rope_tabs = _rope_tables(l)
    gmat = _head_mean_matrix()

    pad_rows = (-(b + 1)) % SUBLANES
    c_rows = jnp.concatenate([c, c_ctx[None], jnp.zeros((pad_rows, d), F32)], axis=0)
    mod = _modulation(c_rows, w_mod, b_mod)

    o_q, o_k, o_v, o_ga = 0, ATTN_WIDTH, ATTN_WIDTH + KV_WIDTH, ATTN_WIDTH + 2 * KV_WIDTH
    o_xbc = o_ga + ATTN_WIDTH
    conv_ch = SSD_INNER + 2 * SSD_GROUPS * SSD_STATE
    o_dt = o_xbc + conv_ch
    o_z = o_dt + 2 * SSD_HEADS
    o_glu = o_z + SSD_INNER
    o_gcv = o_glu + 2 * CM_CH
    o_gm = o_gcv + CM_CH

    xc = ctx
    for i in range(depth):
        last = i == depth - 1
        wi = w_in[i]
        w_kv = wi[:, o_k:o_ga]
        w_attn = jnp.concatenate([w_kv, wi[:, o_q:o_k], wi[:, o_ga:o_xbc]], axis=1).astype(BF16)
        w_ssd = jnp.concatenate([
            wi[:, o_xbc:o_dt],
            _pad_cols(wi[:, o_dt:o_dt + SSD_HEADS], LANES),
            _pad_cols(wi[:, o_dt + SSD_HEADS:o_z], LANES),
            wi[:, o_z:o_glu]], axis=1).astype(BF16)
        w_cm = wi[:, o_glu:o_gm].astype(BF16)
        w_gm = wi[:, o_gm:].astype(BF16)
        dt_bias = jnp.concatenate([_pad_cols(ssd_dt_bias[i, 0:1], LANES),
                                   _pad_cols(ssd_dt_bias[i, 1:2], LANES)], axis=1)
        a_vec = _pad_cols(-jnp.exp(ssd_A_log[i].astype(F32)), LANES)[:, None, :]
        qnw = jnp.tile(q_norm_w[i], ATTN_HEADS)[None]
        knw = jnp.tile(k_norm_w[i], ATTN_KV_HEADS)[None]
        dskip = jnp.repeat(ssd_D[i], SSD_HEAD_DIM)[None]
        nw = norm_w[i][None]

        m_lat = mod[i, :b]
        m_ctx = jnp.broadcast_to(mod[i, b:b + 1], (b, 3 * d))
        split = lambda m: (m[:, None, d:2 * d], m[:, None, :d], m[:, None, 2 * d:])
        sc, sh, gt = split(m_lat)
        sc_c, sh_c, gt_c = split(m_ctx)

        if last:
            k_c, v_c = _attn_in(xc, sc_c, sh_c, nw, w_attn[:, :2 * KV_WIDTH], gmat, qnw, knw,
                                None, False)
        else:
            k_c, v_c, q_c, ga_c = _attn_in(xc, sc_c, sh_c, nw, w_attn, gmat, qnw, knw, None, True)
        k_l, v_l, q_l, ga_l = _attn_in(x, sc, sh, nw, w_attn, gmat, qnw, knw, rope_tabs, True)
        ua = _flash(q_l, jnp.concatenate([k_l, k_c], axis=1),
                    jnp.concatenate([v_l, v_c], axis=1), ga_l)

        ssd_args = (nw, w_ssd, ssd_conv_w[i], ssd_conv_b[i][None], dt_bias)
        xbc_c, dt_c, z_c = _ssd_in(xc, sc_c, sh_c, *ssd_args)
        zero_state = jnp.zeros((b, 2, SSD_STATE, SSD_INNER), F32)
        y_c, state_c = _ssd_scan(xbc_c, dt_c, a_vec, zero_state)
        xbc_l, dt_l, z_l = _ssd_in(x, sc, sh, *ssd_args)
        y_l, _ = _ssd_scan(xbc_l, dt_l, a_vec, state_c)

        cm_args = (nw, w_cm, cm_conv_w[i], cm_conv_b[i][None], cm_ln_w[i][None], cm_ln_b[i][None])
        uc = _conformer(x, sc, sh, *cm_args)

        merge_w = (w_gm, b_gate[i], w_br_attn[i].astype(BF16), w_br_ssd[i].astype(BF16),
                   w_br_conv[i].astype(BF16), w_out[i].astype(BF16), dskip,
                   ssd_norm_w[i][None], final_norm_w[None])
        x_new = _merge(x, sc, sh, gt, nw, ua, y_l, xbc_l, z_l, uc, *merge_w, last)
        if not last:
            ua_c = _flash(q_c, k_c, v_c, ga_c)
            uc_c = _conformer(xc, sc_c, sh_c, *cm_args)
            xc = _merge(xc, sc_c, sh_c, gt_c, nw, ua_c, y_c, xbc_c, z_c, uc_c, *merge_w, False)
        x = x_new
    return x
```

```python
import functools

import numpy as np
import jax
import jax.numpy as jnp
from jax import lax
from jax.experimental import pallas as pl
from jax.experimental.pallas import tpu as pltpu

F32 = jnp.float32
BF16 = jnp.bfloat16

GRID_W = 64
HEAD_DIM = 64
ATTN_HEADS = 8
ATTN_KV_HEADS = 2
ATTN_WIDTH = ATTN_HEADS * HEAD_DIM
KV_WIDTH = ATTN_KV_HEADS * HEAD_DIM
ROPE_THETA = 10000.0
ATTN_SCALE = HEAD_DIM ** -0.5
SSD_HEADS = 8
SSD_HEAD_DIM = 64
SSD_INNER = SSD_HEADS * SSD_HEAD_DIM
SSD_GROUPS = 2
SSD_STATE = 128
SSD_CONV = 5
SSD_CHUNK = 128
CM_CH = 512
CM_KERNEL = 31
N_BRANCH = 3
EPS = 1e-6

LANES = 128
SUBLANES = 8
ROW_TILE = 512
Q_TILE = 1024
Q_ROW_BLOCK = 256
LOG2_E = 1.4426950408889634


def _dot(a, b):
    return jnp.dot(a, b, preferred_element_type=F32)


def _dot_nt(a, b):
    return lax.dot_general(a, b, (((1,), (1,)), ((), ())), preferred_element_type=F32)


def _split3(x):
    x1 = x.astype(BF16)
    r1 = x - x1.astype(F32)
    x2 = r1.astype(BF16)
    r2 = r1 - x2.astype(F32)
    return x1, x2, r2.astype(BF16)


def _split2(x):
    x1 = x.astype(BF16)
    return x1, (x - x1.astype(F32)).astype(BF16)


def _dot_split_lhs(x, m):
    x1, x2 = _split2(x)
    return _dot(x1, m) + _dot(x2, m)


def _dot_split_rhs(m, x):
    x1, x2 = _split2(x)
    return _dot(m, x1) + _dot(m, x2)


def _sigmoid(x):
    return 1.0 / (1.0 + jnp.exp(-x))


def _silu(x):
    return x * _sigmoid(x)


def _softplus(x):
    return jnp.maximum(x, 0.0) + jnp.log1p(jnp.exp(-jnp.abs(x)))


def _modulated_norm(x, nw, scale, shift):
    ms = jnp.mean(x * x, axis=-1, keepdims=True)
    return x * lax.rsqrt(ms + EPS) * (nw * (1.0 + scale)) + shift


def _rms(x, w):
    ms = jnp.mean(x * x, axis=-1, keepdims=True)
    return x * lax.rsqrt(ms + EPS) * w


def _mod_kernel(c_ref, w_ref, b_ref, o_ref):
    a1, a2, a3 = _split3(_silu(c_ref[...]))
    w1, w2, w3 = _split3(w_ref[0])
    o = (_dot(a1, w1) + _dot(a1, w2) + _dot(a2, w1)
         + _dot(a2, w2) + _dot(a1, w3) + _dot(a3, w1))
    o_ref[0] = o + b_ref[0]


def _modulation(c_rows, w_mod, b_mod):
    depth, d, d3 = w_mod.shape
    rows = c_rows.shape[0]
    col_tile = 768
    return pl.pallas_call(
        _mod_kernel,
        out_shape=jax.ShapeDtypeStruct((depth, rows, d3), F32),
        grid=(depth, d3 // col_tile),
        in_specs=[
            pl.BlockSpec((rows, d), lambda i, j: (0, 0)),
            pl.BlockSpec((1, d, col_tile), lambda i, j: (i, 0, j)),
            pl.BlockSpec((1, 1, col_tile), lambda i, j: (i, 0, j)),
        ],
        out_specs=pl.BlockSpec((1, rows, col_tile), lambda i, j: (i, 0, j)),
        name="modulation",
    )(c_rows, w_mod, b_mod.reshape(depth, 1, d3))


def _head_norm(t, gmat, w):
    ms = _dot((t * t).astype(BF16), gmat)
    return t * lax.rsqrt(ms + EPS) * w


def _rope(t, cos, sin_lo, sin_hi):
    outs = []
    for j in range(t.shape[1] // LANES):
        tj = t[:, LANES * j:LANES * (j + 1)]
        outs.append(tj * cos + pltpu.roll(tj, LANES - 16, 1) * sin_lo
                    + pltpu.roll(tj, 16, 1) * sin_hi)
    return outs[0] if len(outs) == 1 else jnp.concatenate(outs, axis=1)


def _attn_in_kernel(*refs, rope, need_q):
    if rope:
        (x_ref, sc_ref, sh_ref, nw_ref, w_ref, g_ref, qnw_ref, knw_ref,
         cos_ref, slo_ref, shi_ref, *outs) = refs
    else:
        (x_ref, sc_ref, sh_ref, nw_ref, w_ref, g_ref, qnw_ref, knw_ref, *outs) = refs
    h = _modulated_norm(x_ref[0], nw_ref[...], sc_ref[0], sh_ref[0]).astype(BF16)
    p = _dot(h, w_ref[...])
    k = _head_norm(p[:, :KV_WIDTH], g_ref[:KV_WIDTH, :KV_WIDTH], knw_ref[...])
    if rope:
        k = _rope(k, cos_ref[...], slo_ref[...], shi_ref[...])
    if need_q:
        k_ref, v_ref, q_ref, ga_ref = outs
    else:
        k_ref, v_ref = outs
    k_ref[0] = k.astype(BF16)
    v = p[:, KV_WIDTH:2 * KV_WIDTH]
    first_half = lax.broadcasted_iota(jnp.int32, (1, KV_WIDTH), 1) < HEAD_DIM
    v_ref[0] = jnp.concatenate([jnp.where(first_half, v, 1.0), jnp.where(first_half, 1.0, v)],
                               axis=1).astype(BF16)
    if need_q:
        q0 = 2 * KV_WIDTH
        q = _head_norm(p[:, q0:q0 + ATTN_WIDTH], g_ref[...], qnw_ref[...])
        if rope:
            q = _rope(q, cos_ref[...], slo_ref[...], shi_ref[...])
        q_ref[0] = (q * (ATTN_SCALE * LOG2_E)).astype(BF16)
        ga_ref[0] = _silu(p[:, q0 + ATTN_WIDTH:])


def _attn_in(x, scale, shift, nw, w, gmat, qnw, knw, rope_tabs, need_q):
    b, l, d = x.shape
    tm = min(ROW_TILE, l)
    rope = rope_tabs is not None
    cols = w.shape[1]
    vec = lambda n: pl.BlockSpec((1, n), lambda bi, i: (0, 0))
    in_specs = [
        pl.BlockSpec((1, tm, d), lambda bi, i: (bi, i, 0)),
        pl.BlockSpec((1, 1, d), lambda bi, i: (bi, 0, 0)),
        pl.BlockSpec((1, 1, d), lambda bi, i: (bi, 0, 0)),
        vec(d),
        pl.BlockSpec((d, cols), lambda bi, i: (0, 0)),
        pl.BlockSpec((ATTN_WIDTH, ATTN_WIDTH), lambda bi, i: (0, 0)),
        vec(ATTN_WIDTH),
        vec(KV_WIDTH),
    ]
    args = [x, scale, shift, nw, w, gmat, qnw, knw]
    if rope:
        in_specs += [pl.BlockSpec((tm, LANES), lambda bi, i: (i, 0))] * 3
        args += list(rope_tabs)
    tok = lambda n: pl.BlockSpec((1, tm, n), lambda bi, i: (bi, i, 0))
    out_shape = [jax.ShapeDtypeStruct((b, l, KV_WIDTH), BF16),
                 jax.ShapeDtypeStruct((b, l, 2 * KV_WIDTH), BF16)]
    out_specs = [tok(KV_WIDTH), tok(2 * KV_WIDTH)]
    if need_q:
        out_shape += [jax.ShapeDtypeStruct((b, l, ATTN_WIDTH), BF16),
                      jax.ShapeDtypeStruct((b, l, ATTN_WIDTH), F32)]
        out_specs += [tok(ATTN_WIDTH), tok(ATTN_WIDTH)]
    return pl.pallas_call(
        functools.partial(_attn_in_kernel, rope=rope, need_q=need_q),
        out_shape=out_shape,
        grid=(b, l // tm),
        in_specs=in_specs,
        out_specs=out_specs,
        name="attn_in",
    )(*args)


def _flash_kernel(q_ref, k_ref, v_ref, ga_ref, o_ref, *, row_block):
    group = pl.program_id(2) // 2
    k = k_ref[0]
    v = v_ref[0]
    lane_half = lax.broadcasted_iota(jnp.int32, (1, LANES), 1) // HEAD_DIM
    in_group = lane_half == group
    for rb in range(q_ref.shape[1] // row_block):
        rows = pl.ds(rb * row_block, row_block)
        q2 = q_ref[0, rows, :].astype(F32)
        q2_swapped = pltpu.roll(q2, HEAD_DIM, 1)
        heads = []
        for hh in range(2):
            src = jnp.where(group == hh, q2, q2_swapped)
            qp = jnp.where(in_group, src, 0.0).astype(BF16)
            s = _dot_nt(qp, k)
            p = jnp.exp2(s - jnp.max(s, axis=-1, keepdims=True))
            o = _dot(p.astype(BF16), v)
            o = o / pltpu.roll(o, HEAD_DIM, 1)
            heads.append(jnp.where(group == hh, o, pltpu.roll(o, HEAD_DIM, 1)))
        o2 = jnp.where(lane_half == 0, heads[0], heads[1])
        o_ref[0, rows, :] = o2 * ga_ref[0, rows, :]


def _flash(q, k, v, ga):
    b, l, _ = q.shape
    s = k.shape[1]
    tq = min(Q_TILE, l)
    return pl.pallas_call(
        functools.partial(_flash_kernel, row_block=min(Q_ROW_BLOCK, tq)),
        out_shape=jax.ShapeDtypeStruct((b, l, ATTN_WIDTH), F32),
        grid=(b, l // tq, ATTN_WIDTH // LANES),
        in_specs=[
            pl.BlockSpec((1, tq, LANES), lambda bi, i, m: (bi, i, m)),
            pl.BlockSpec((1, s, KV_WIDTH), lambda bi, i, m: (bi, 0, 0)),
            pl.BlockSpec((1, s, LANES), lambda bi, i, m: (bi, 0, m // 2)),
            pl.BlockSpec((1, tq, LANES), lambda bi, i, m: (bi, i, m)),
        ],
        out_specs=pl.BlockSpec((1, tq, LANES), lambda bi, i, m: (bi, i, m)),
        name="flash",
    )(q, k, v, ga)


def _halo_rows_valid(rows, halo, tile, i, n_tiles):
    r = lax.broadcasted_iota(jnp.int32, (rows, 1), 0)
    head_ok = jnp.where(i > 0, 0, halo)
    tail_ok = jnp.where(i < n_tiles - 1, rows, halo + tile)
    return (r >= head_ok) & (r < tail_ok)


def _ssd_in_kernel(xp_ref, x_ref, xn_ref, sc_ref, sh_ref, nw_ref, w_ref, cw_ref, cb_ref,
                   dtb_ref, xbc_ref, dt_ref, z_ref, *, n_tiles):
    i = pl.program_id(1)
    t = x_ref.shape[1]
    halo = SUBLANES
    xe = jnp.concatenate([xp_ref[0], x_ref[0], xn_ref[0]], axis=0)
    h = _modulated_norm(xe, nw_ref[...], sc_ref[0], sh_ref[0]).astype(BF16)
    conv_ch = xbc_ref.shape[2]
    raw = _dot(h, w_ref[:, :conv_ch])
    raw = jnp.where(_halo_rows_valid(t + 2 * halo, halo, t, i, n_tiles), raw, 0.0)
    acc = jnp.broadcast_to(cb_ref[...], (t, conv_ch))
    rows = t + 2 * halo
    for k in range(SSD_CONV):
        back = SSD_CONV // 2 - k
        shifted = raw if back == 0 else pltpu.roll(raw, back % rows, 0)
        acc = acc + cw_ref[k:k + 1, :] * shifted[halo:halo + t, :]
    xbc_ref[0] = _silu(acc)
    rest = _dot(h[halo:halo + t], w_ref[:, conv_ch:])
    dt_ref[0] = _softplus(rest[:, :2 * LANES] + dtb_ref[...])
    z_ref[0] = rest[:, 2 * LANES:]


def _halo_specs(tm, d, halo, l):
    per = tm // halo
    last = l // halo - 1
    prev = pl.BlockSpec((1, halo, d), lambda bi, i: (bi, jnp.maximum(i * per - 1, 0), 0))
    cur = pl.BlockSpec((1, tm, d), lambda bi, i: (bi, i, 0))
    nxt = pl.BlockSpec((1, halo, d), lambda bi, i: (bi, jnp.minimum((i + 1) * per, last), 0))
    return [prev, cur, nxt]


def _ssd_in(x, scale, shift, nw, w, conv_w, conv_b, dt_bias):
    b, l, d = x.shape
    tm = min(ROW_TILE, l)
    n_tiles = l // tm
    cols = w.shape[1]
    conv_ch = conv_w.shape[1]
    const = lambda shp: pl.BlockSpec(shp, lambda bi, i: (0,) * len(shp))
    tok = lambda n: pl.BlockSpec((1, tm, n), lambda bi, i: (bi, i, 0))
    return pl.pallas_call(
        functools.partial(_ssd_in_kernel, n_tiles=n_tiles),
        out_shape=[jax.ShapeDtypeStruct((b, l, conv_ch), F32),
                   jax.ShapeDtypeStruct((b, l, 2 * LANES), F32),
                   jax.ShapeDtypeStruct((b, l, SSD_INNER), F32)],
        grid=(b, n_tiles),
        in_specs=_halo_specs(tm, d, SUBLANES, l) + [
            pl.BlockSpec((1, 1, d), lambda bi, i: (bi, 0, 0)),
            pl.BlockSpec((1, 1, d), lambda bi, i: (bi, 0, 0)),
            const((1, d)), const((d, cols)), const(conv_w.shape), const((1, conv_ch)),
            const((1, 2 * LANES)),
        ],
        out_specs=[tok(conv_ch), tok(2 * LANES), tok(SSD_INNER)],
        name="ssd_in",
    )(x, x, x, scale, shift, nw, w, conv_w, conv_b, dt_bias)


def _scan_chunks(chains, expand):
    n = len(chains)
    q = chains[0][0].shape[0]
    gw = SSD_INNER // SSD_GROUPS
    groups = range(SSD_GROUPS)
    gsl = lambda t, g, w: t[:, w * g:w * (g + 1)]
    c_bf = [[gsl(ch[2], g, SSD_STATE).astype(BF16) for g in groups] for ch in chains]
    b_t = [[gsl(ch[1], g, SSD_STATE).T.astype(BF16) for g in groups] for ch in chains]
    st_bf = [ch[7].astype(BF16) for ch in chains]
    cb = [[_dot(c_bf[i][g], b_t[i][g]) for g in groups] for i in range(n)]
    y_off = [[_dot(c_bf[i][g], gsl(st_bf[i], g, gw)) for g in groups] for i in range(n)]
    a = [ch[3] * ch[4] for ch in chains]
    cum = [_dot_split_rhs(ch[5], a[i]) for i, ch in enumerate(chains)]
    wide = [_dot_split_lhs(jnp.concatenate([cum[i], ch[3]], axis=0), expand)
            for i, ch in enumerate(chains)]
    rows_tot = cum[0].shape[0] - q
    out = []
    lane_lo = lax.broadcasted_iota(jnp.int32, (1, LANES), 1) < SSD_HEAD_DIM
    for i, ch in enumerate(chains):
        xs, mask, st = ch[0], ch[6], ch[7]
        acum = cum[i][:q]
        acum_w = wide[i][:q]
        tot_w = wide[i][q:q + 1]
        xd = xs * wide[i][q + rows_tot:]
        xdd = (xd * jnp.exp(tot_w - acum_w)).astype(BF16)
        xdb = xd.astype(BF16)
        acum_t = acum.T
        upd = [_dot(b_t[i][g], gsl(xdd, g, gw)) for g in groups]
        y_diag = []
        for blk in range(SSD_INNER // LANES):
            g = blk // (gw // LANES)
            xpair = gsl(xdb, blk, LANES)
            res = []
            for hh in range(2):
                head = 2 * blk + hh
                seg = acum[:, head:head + 1] - acum_t[head:head + 1, :]
                lmat = jnp.exp(jnp.where(mask, seg, -jnp.inf))
                res.append(_dot((cb[i][g] * lmat).astype(BF16), xpair))
            y_diag.append(jnp.where(lane_lo, res[0], res[1]))
        y = jnp.concatenate(y_diag, axis=1) + jnp.concatenate(y_off[i], axis=1) * jnp.exp(acum_w)
        out.append((y, jnp.exp(tot_w) * st + jnp.concatenate(upd, axis=1)))
    return out


def _ssd_scan_kernel(xf_ref, bf_ref, cf_ref, dtf_ref, xb_ref, bb_ref, cb_ref, dtb_ref,
                     a_ref, tri_ref, mask_ref, e_ref, init_ref, yf_ref, yb_ref, fin_ref, st_ref):
    @pl.when(pl.program_id(1) == 0)
    def _():
        st_ref[...] = init_ref[0]

    streams = ((xf_ref, bf_ref, cf_ref, dtf_ref), (xb_ref, bb_ref, cb_ref, dtb_ref))
    chains = [(x_ref[0], b_ref[0], c_ref[0], dt_ref[0], a_ref[d], tri_ref[d], mask_ref[d] > 0.5,
               st_ref[d]) for d, (x_ref, b_ref, c_ref, dt_ref) in enumerate(streams)]
    results = _scan_chunks(chains, e_ref[...])
    for d, (y_ref, (y, st_new)) in enumerate(zip((yf_ref, yb_ref), results)):
        y_ref[0] = y
        st_ref[d] = st_new
        fin_ref[0, d] = st_new


def _scan_constants(q):
    r = np.arange(q)
    lower = (r[:, None] >= r[None, :]).astype(np.float32)
    ones = np.ones((SUBLANES, q), np.float32)
    tri = np.stack([np.concatenate([lower, ones], 0), np.concatenate([lower.T, ones], 0)])
    mask = np.stack([lower, lower.T])
    expand = np.zeros((LANES, SSD_INNER), np.float32)
    for hd in range(SSD_HEADS):
        expand[hd, SSD_HEAD_DIM * hd:SSD_HEAD_DIM * (hd + 1)] = 1.0
    return jnp.asarray(tri, BF16), jnp.asarray(mask, F32), jnp.asarray(expand, BF16)


def _ssd_scan(xbc, dt, a_vec, init):
    b, l, _ = xbc.shape
    q = SSD_CHUNK
    nc = l // q
    tri, mask, expand = _scan_constants(q)
    bc_w = SSD_GROUPS * SSD_STATE
    bc_blk = SSD_INNER // bc_w
    const = lambda shp: pl.BlockSpec(shp, lambda bi, c: (0,) * len(shp))

    def stream(direction):
        chunk = (lambda c: c) if direction == 0 else (lambda c: nc - 1 - c)
        return [
            pl.BlockSpec((1, q, SSD_INNER), lambda bi, c: (bi, chunk(c), 0)),
            pl.BlockSpec((1, q, bc_w), lambda bi, c: (bi, chunk(c), bc_blk)),
            pl.BlockSpec((1, q, bc_w), lambda bi, c: (bi, chunk(c), bc_blk + 1)),
            pl.BlockSpec((1, q, LANES), lambda bi, c: (bi, chunk(c), direction)),
        ], pl.BlockSpec((1, q, SSD_INNER), lambda bi, c: (bi, chunk(c), 0))

    in_f, out_f = stream(0)
    in_b, out_b = stream(1)
    state_spec = pl.BlockSpec((1, 2, SSD_STATE, SSD_INNER), lambda bi, c: (bi, 0, 0, 0))
    return pl.pallas_call(
        _ssd_scan_kernel,
        out_shape=[jax.ShapeDtypeStruct((b, l, SSD_INNER), F32),
                   jax.ShapeDtypeStruct((b, l, SSD_INNER), F32),
                   jax.ShapeDtypeStruct((b, 2, SSD_STATE, SSD_INNER), F32)],
        grid=(b, nc),
        in_specs=in_f + in_b + [const(a_vec.shape), const(tri.shape), const(mask.shape),
                                const(expand.shape), state_spec],
        out_specs=[out_f, out_b, state_spec],
        scratch_shapes=[pltpu.VMEM((2, SSD_STATE, SSD_INNER), F32)],
        name="ssd_scan",
    )(xbc, xbc, xbc, dt, xbc, xbc, xbc, dt, a_vec, tri, mask, expand, init)


def _conv_kernel(xp_ref, x_ref, xn_ref, sc_ref, sh_ref, nw_ref, w_ref, cw_ref, cb_ref,
                 lnw_ref, lnb_ref, o_ref, *, n_tiles):
    i = pl.program_id(1)
    t = x_ref.shape[1]
    halo = 2 * SUBLANES
    ch = o_ref.shape[2]
    xe = jnp.concatenate([xp_ref[0], x_ref[0], xn_ref[0]], axis=0)
    h = _modulated_norm(xe, nw_ref[...], sc_ref[0], sh_ref[0]).astype(BF16)
    ug = _dot(h, w_ref[:, :2 * ch])
    v = ug[:, :ch] * _sigmoid(ug[:, ch:])
    v = jnp.where(_halo_rows_valid(t + 2 * halo, halo, t, i, n_tiles), v, 0.0)
    first = halo - CM_KERNEL // 2
    acc = jnp.broadcast_to(cb_ref[...], (t, ch))
    for r in range(SUBLANES):
        part = None
        for al in range(0, 2 * halo, SUBLANES):
            k = al + r - first
            if 0 <= k < CM_KERNEL:
                term = cw_ref[k:k + 1, :] * v[al:al + t + SUBLANES]
                part = term if part is None else part + term
        acc = acc + part[r:r + t]
    mu = jnp.mean(acc, axis=-1, keepdims=True)
    dev = acc - mu
    var = jnp.mean(dev * dev, axis=-1, keepdims=True)
    y = _silu(dev * lax.rsqrt(var + EPS) * lnw_ref[...] + lnb_ref[...])
    gate = _dot(h[halo:halo + t], w_ref[:, 2 * ch:])
    o_ref[0] = y * _silu(gate)


def _conformer(x, scale, shift, nw, w, conv_w, conv_b, ln_w, ln_b):
    b, l, d = x.shape
    tm = min(ROW_TILE, l)
    n_tiles = l // tm
    ch = conv_w.shape[1]
    const = lambda shp: pl.BlockSpec(shp, lambda bi, i: (0,) * len(shp))
    return pl.pallas_call(
        functools.partial(_conv_kernel, n_tiles=n_tiles),
        out_shape=jax.ShapeDtypeStruct((b, l, ch), F32),
        grid=(b, n_tiles),
        in_specs=_halo_specs(tm, d, 2 * SUBLANES, l) + [
            pl.BlockSpec((1, 1, d), lambda bi, i: (bi, 0, 0)),
            pl.BlockSpec((1, 1, d), lambda bi, i: (bi, 0, 0)),
            const((1, d)), const(w.shape), const(conv_w.shape), const((1, ch)),
            const((1, ch)), const((1, ch)),
        ],
        out_specs=pl.BlockSpec((1, tm, ch), lambda bi, i: (bi, i, 0)),
        name="conformer",
    )(x, x, x, scale, shift, nw, w, conv_w, conv_b, ln_w, ln_b)


def _merge_kernel(x_ref, sc_ref, sh_ref, gt_ref, nw_ref, ua_ref, yf_ref, yb_ref, xs_ref, z_ref,
                  uc_ref, wgm_ref, bg_ref, wa_ref, ws_ref, wc_ref, wo_ref, dsk_ref, snw_ref,
                  fnw_ref, o_ref, *, last):
    x = x_ref[0]
    d = x.shape[1]
    h = _modulated_norm(x, nw_ref[...], sc_ref[0], sh_ref[0]).astype(BF16)
    y = yf_ref[0] + yb_ref[0] + dsk_ref[...] * xs_ref[0]
    us = _rms(y * _silu(z_ref[0]), snw_ref[...])
    acc = None
    for j, (u, w_ref) in enumerate(((ua_ref[0], wa_ref), (us, ws_ref), (uc_ref[0], wc_ref))):
        gate = _sigmoid(_dot(h, wgm_ref[:, d * j:d * (j + 1)]) + bg_ref[j:j + 1, :])
        term = gate * _dot(u.astype(BF16), w_ref[...])
        acc = term if acc is None else acc + term
    out = _dot(acc.astype(BF16), wo_ref[...])
    xn = x + gt_ref[0] * out
    if last:
        xn = _rms(xn, fnw_ref[...])
    o_ref[0] = xn


def _merge(x, scale, shift, gate, nw, ua, yf, yb, xbc, z, uc, wgm, bgate, wa, ws, wc, wo,
           dskip, snw, fnw, last):
    b, l, d = x.shape
    tm = min(ROW_TILE // 2, l)
    const = lambda shp: pl.BlockSpec(shp, lambda bi, i: (0,) * len(shp))
    tok = lambda n: pl.BlockSpec((1, tm, n), lambda bi, i: (bi, i, 0))
    per_b = pl.BlockSpec((1, 1, d), lambda bi, i: (bi, 0, 0))
    return pl.pallas_call(
        functools.partial(_merge_kernel, last=last),
        out_shape=jax.ShapeDtypeStruct((b, l, d), F32),
        grid=(b, l // tm),
        in_specs=[
            tok(d), per_b, per_b, per_b, const((1, d)),
            tok(ATTN_WIDTH), tok(SSD_INNER), tok(SSD_INNER),
            tok(SSD_INNER), tok(SSD_INNER), tok(CM_CH),
            const(wgm.shape), const(bgate.shape), const(wa.shape), const(ws.shape),
            const(wc.shape), const(wo.shape), const((1, SSD_INNER)), const((1, SSD_INNER)),
            const((1, d)),
        ],
        out_specs=tok(d),
        name="merge",
    )(x, scale, shift, gate, nw, ua, yf, yb, xbc, z, uc, wgm, bgate, wa, ws, wc, wo,
      dskip, snw, fnw)


def _rope_tables(length):
    rows = length // GRID_W
    row = jnp.repeat(jnp.arange(rows), GRID_W).astype(F32)
    col = jnp.tile(jnp.arange(GRID_W), rows).astype(F32)
    n_freq = HEAD_DIM // 4
    inv = 1.0 / (ROPE_THETA ** (jnp.arange(n_freq, dtype=F32) / n_freq))
    ang_r = row[:, None] * inv
    ang_c = col[:, None] * inv
    zero = jnp.zeros_like(ang_r)
    cos_head = jnp.concatenate([jnp.cos(ang_r)] * 2 + [jnp.cos(ang_c)] * 2, axis=1)
    lo_head = jnp.concatenate([-jnp.sin(ang_r), zero, -jnp.sin(ang_c), zero], axis=1)
    hi_head = jnp.concatenate([zero, jnp.sin(ang_r), zero, jnp.sin(ang_c)], axis=1)
    two = lambda t: jnp.concatenate([t, t], axis=1)
    return two(cos_head), two(lo_head), two(hi_head)


def _head_mean_matrix():
    idx = np.arange(ATTN_WIDTH) // HEAD_DIM
    return jnp.asarray((idx[:, None] == idx[None, :]).astype(np.float32) / HEAD_DIM, BF16)


def _pad_cols(a, width):
    return jnp.pad(a, ((0, 0), (0, width - a.shape[1])))


def kernel(x, c, ctx, c_ctx, w_mod, b_mod, norm_w, w_in, q_norm_w, k_norm_w, ssd_conv_w,
           ssd_conv_b, ssd_A_log, ssd_dt_bias, ssd_D, ssd_norm_w, cm_conv_w, cm_conv_b,
           cm_ln_w, cm_ln_b, w_br_attn, w_br_ssd, w_br_conv, b_gate, w_out, final_norm_w):
    b, l, d = x.shape
    depth = w_mod.shape[0]
    rope_tabs = _rope_tables(l)
    gmat = _head_mean_matrix()

    pad_rows = (-(b + 1)) % SUBLANES
    c_rows = jnp.concatenate([c, c_ctx[None], jnp.zeros((pad_rows, d), F32)], axis=0)
    mod = _modulation(c_rows, w_mod, b_mod)

    o_q, o_k, o_v, o_ga = 0, ATTN_WIDTH, ATTN_WIDTH + KV_WIDTH, ATTN_WIDTH + 2 * KV_WIDTH
    o_xbc = o_ga + ATTN_WIDTH
    conv_ch = SSD_INNER + 2 * SSD_GROUPS * SSD_STATE
    o_dt = o_xbc + conv_ch
    o_z = o_dt + 2 * SSD_HEADS
    o_glu = o_z + SSD_INNER
    o_gcv = o_glu + 2 * CM_CH
    o_gm = o_gcv + CM_CH

    xc = ctx
    for i in range(depth):
        last = i == depth - 1
        wi = w_in[i]
        w_kv = wi[:, o_k:o_ga]
        w_attn = jnp.concatenate([w_kv, wi[:, o_q:o_k], wi[:, o_ga:o_xbc]], axis=1).astype(BF16)
        w_ssd = jnp.concatenate([
            wi[:, o_xbc:o_dt],
            _pad_cols(wi[:, o_dt:o_dt + SSD_HEADS], LANES),
            _pad_cols(wi[:, o_dt + SSD_HEADS:o_z], LANES),
            wi[:, o_z:o_glu]], axis=1).astype(BF16)
        w_cm = wi[:, o_glu:o_gm].astype(BF16)
        w_gm = wi[:, o_gm:].astype(BF16)
        dt_bias = jnp.concatenate([_pad_cols(ssd_dt_bias[i, 0:1], LANES),
                                   _pad_cols(ssd_dt_bias[i, 1:2], LANES)], axis=1)
        a_vec = _pad_cols(-jnp.exp(ssd_A_log[i].astype(F32)), LANES)[:, None, :]
        qnw = jnp.tile(q_norm_w[i], ATTN_HEADS)[None]
        knw = jnp.tile(k_norm_w[i], ATTN_KV_HEADS)[None]
        dskip = jnp.repeat(ssd_D[i], SSD_HEAD_DIM)[None]
        nw = norm_w[i][None]

        m_lat = mod[i, :b]
        m_ctx = jnp.broadcast_to(mod[i, b:b + 1], (b, 3 * d))
        split = lambda m: (m[:, None, d:2 * d], m[:, None, :d], m[:, None, 2 * d:])
        sc, sh, gt = split(m_lat)
        sc_c, sh_c, gt_c = split(m_ctx)

        if last:
            k_c, v_c = _attn_in(xc, sc_c, sh_c, nw, w_attn[:, :2 * KV_WIDTH], gmat, qnw, knw,
                                None, False)
        else:
            k_c, v_c, q_c, ga_c = _attn_in(xc, sc_c, sh_c, nw, w_attn, gmat, qnw, knw, None, True)
        k_l, v_l, q_l, ga_l = _attn_in(x, sc, sh, nw, w_attn, gmat, qnw, knw, rope_tabs, True)
        ua = _flash(q_l, jnp.concatenate([k_l, k_c], axis=1),
                    jnp.concatenate([v_l, v_c], axis=1), ga_l)

        ssd_args = (nw, w_ssd, ssd_conv_w[i], ssd_conv_b[i][None], dt_bias)
        xbc_c, dt_c, z_c = _ssd_in(xc, sc_c, sh_c, *ssd_args)
        zero_state = jnp.zeros((b, 2, SSD_STATE, SSD_INNER), F32)
        yf_c, yb_c, state_c = _ssd_scan(xbc_c, dt_c, a_vec, zero_state)
        xbc_l, dt_l, z_l = _ssd_in(x, sc, sh, *ssd_args)
        yf_l, yb_l, _ = _ssd_scan(xbc_l, dt_l, a_vec, state_c)

        cm_args = (nw, w_cm, cm_conv_w[i], cm_conv_b[i][None], cm_ln_w[i][None], cm_ln_b[i][None])
        uc = _conformer(x, sc, sh, *cm_args)

        merge_w = (w_gm, b_gate[i], w_br_attn[i].astype(BF16), w_br_ssd[i].astype(BF16),
                   w_br_conv[i].astype(BF16), w_out[i].astype(BF16), dskip,
                   ssd_norm_w[i][None], final_norm_w[None])
        x_new = _merge(x, sc, sh, gt, nw, ua, yf_l, yb_l, xbc_l, z_l, uc, *merge_w, last)
        if not last:
            ua_c = _flash(q_c, k_c, v_c, ga_c)
            uc_c = _conformer(xc, sc_c, sh_c, *cm_args)
            xc = _merge(xc, sc_c, sh_c, gt_c, nw, ua_c, yf_c, yb_c, xbc_c, z_c, uc_c, *merge_w,
                        False)
        x = x_new
    return x
```

```python
import functools

import numpy as np
import jax
import jax.numpy as jnp
from jax import lax
from jax.experimental import pallas as pl
from jax.experimental.pallas import tpu as pltpu

F32 = jnp.float32
BF16 = jnp.bfloat16

GRID_W = 64
HEAD_DIM = 64
ATTN_HEADS = 8
ATTN_KV_HEADS = 2
ATTN_WIDTH = ATTN_HEADS * HEAD_DIM
KV_WIDTH = ATTN_KV_HEADS * HEAD_DIM
ROPE_THETA = 10000.0
ATTN_SCALE = HEAD_DIM ** -0.5
SSD_HEADS = 8
SSD_HEAD_DIM = 64
SSD_INNER = SSD_HEADS * SSD_HEAD_DIM
SSD_GROUPS = 2
SSD_STATE = 128
SSD_CONV = 5
SSD_CHUNK = 128
CM_CH = 512
CM_KERNEL = 31
N_BRANCH = 3
EPS = 1e-6
LOG2_E = 1.4426950408889634

LANES = 128
SUBLANES = 8
BF16_ROWS = 16
HALO = BF16_ROWS
ROW_TILE = 512
Q_TILE = 1024
Q_BLOCK = 512
SCAN_CHUNKS_PER_STEP = 4
MAX_ROWS = 256
V_ROWS = HEAD_DIM + BF16_ROWS


def _dot(a, b):
    return jnp.dot(a, b, preferred_element_type=F32)


def _split2(x):
    x1 = x.astype(BF16)
    return x1, (x - x1.astype(F32)).astype(BF16)


def _split3(x):
    x1 = x.astype(BF16)
    r1 = x - x1.astype(F32)
    x2 = r1.astype(BF16)
    r2 = r1 - x2.astype(F32)
    return x1, x2, r2.astype(BF16)


def _dot_split_lhs(x, m):
    x1, x2 = _split2(x)
    return _dot(x1, m) + _dot(x2, m)


def _dot_split_rhs(m, x):
    x1, x2 = _split2(x)
    return _dot(m, x1) + _dot(m, x2)


def _sigmoid(x):
    return 1.0 / (1.0 + jnp.exp(-x))


def _silu(x):
    return x * _sigmoid(x)


def _softplus(x):
    return jnp.maximum(x, 0.0) + jnp.log1p(jnp.exp(-jnp.abs(x)))


def _modulated_norm(x, nw, scale, shift):
    ms = jnp.mean(x * x, axis=-1, keepdims=True)
    return x * lax.rsqrt(ms + EPS) * (nw * (1.0 + scale)) + shift


def _rms(x, w):
    ms = jnp.mean(x * x, axis=-1, keepdims=True)
    return x * lax.rsqrt(ms + EPS) * w


def _mod_kernel(c_ref, w_ref, b_ref, o_ref):
    a1, a2, a3 = _split3(_silu(c_ref[...]))
    w1, w2, w3 = _split3(w_ref[0])
    o = (_dot(a1, w1) + _dot(a1, w2) + _dot(a2, w1)
         + _dot(a2, w2) + _dot(a1, w3) + _dot(a3, w1))
    o_ref[0] = o + b_ref[0]


def _modulation(c_rows, w_mod, b_mod):
    depth, d, d3 = w_mod.shape
    rows = c_rows.shape[0]
    col_tile = 768
    return pl.pallas_call(
        _mod_kernel,
        out_shape=jax.ShapeDtypeStruct((depth, rows, d3), F32),
        grid=(depth, d3 // col_tile),
        in_specs=[
            pl.BlockSpec((rows, d), lambda i, j: (0, 0)),
            pl.BlockSpec((1, d, col_tile), lambda i, j: (i, 0, j)),
            pl.BlockSpec((1, 1, col_tile), lambda i, j: (i, 0, j)),
        ],
        out_specs=pl.BlockSpec((1, rows, col_tile), lambda i, j: (i, 0, j)),
        name="modulation",
    )(c_rows, w_mod, b_mod.reshape(depth, 1, d3))


def _hnorm_kernel(x_ref, sc_ref, sh_ref, nw_ref, h_ref):
    h_ref[0] = _modulated_norm(x_ref[0], nw_ref[...], sc_ref[0], sh_ref[0]).astype(BF16)


def _hnorm(x, scale, shift, nw):
    b, l, d = x.shape
    tm = min(ROW_TILE, l)
    per_b = pl.BlockSpec((1, 1, d), lambda bi, i: (bi, 0, 0))
    tok = pl.BlockSpec((1, tm, d), lambda bi, i: (bi, i, 0))
    return pl.pallas_call(
        _hnorm_kernel,
        out_shape=jax.ShapeDtypeStruct((b, l, d), BF16),
        grid=(b, l // tm),
        in_specs=[tok, per_b, per_b, pl.BlockSpec((1, d), lambda bi, i: (0, 0))],
        out_specs=tok,
        name="hnorm",
    )(x, scale, shift, nw)


def _head_norm(t, gmat, w):
    ms = _dot((t * t).astype(BF16), gmat)
    return t * lax.rsqrt(ms + EPS) * w


def _rope(t, cos, sin_lo, sin_hi):
    outs = []
    for j in range(t.shape[1] // LANES):
        tj = t[:, LANES * j:LANES * (j + 1)]
        outs.append(tj * cos + pltpu.roll(tj, LANES - 16, 1) * sin_lo
                    + pltpu.roll(tj, 16, 1) * sin_hi)
    return outs[0] if len(outs) == 1 else jnp.concatenate(outs, axis=1)


def _attn_in_kernel(*refs, rope, need_q):
    if rope:
        h_ref, w_ref, g_ref, qnw_ref, knw_ref, cos_ref, slo_ref, shi_ref, *outs = refs
    else:
        h_ref, w_ref, g_ref, qnw_ref, knw_ref, *outs = refs
    p = _dot(h_ref[0], w_ref[...])
    k = _head_norm(p[:, :KV_WIDTH], g_ref[:KV_WIDTH, :KV_WIDTH], knw_ref[...])
    if rope:
        k = _rope(k, cos_ref[...], slo_ref[...], shi_ref[...])
    if need_q:
        k_ref, vt_ref, qt_ref, ga_ref = outs
    else:
        k_ref, vt_ref = outs
    k_ref[0] = k.astype(BF16)
    vt = p[:, KV_WIDTH:2 * KV_WIDTH].T
    ones = jnp.ones((V_ROWS - HEAD_DIM, vt.shape[1]), F32)
    for g in range(ATTN_KV_HEADS):
        vt_ref[0, g] = jnp.concatenate([vt[HEAD_DIM * g:HEAD_DIM * (g + 1)], ones],
                                       axis=0).astype(BF16)
    if need_q:
        q0 = 2 * KV_WIDTH
        q = _head_norm(p[:, q0:q0 + ATTN_WIDTH], g_ref[...], qnw_ref[...])
        if rope:
            q = _rope(q, cos_ref[...], slo_ref[...], shi_ref[...])
        qt_ref[0] = (q * (ATTN_SCALE * LOG2_E)).T.astype(BF16)
        ga_ref[0] = _silu(p[:, q0 + ATTN_WIDTH:])


def _attn_in(h, w, gmat, qnw, knw, rope_tabs, need_q):
    b, l, d = h.shape
    tm = min(ROW_TILE, l)
    rope = rope_tabs is not None
    const = lambda shp: pl.BlockSpec(shp, lambda bi, i: (0,) * len(shp))
    in_specs = [
        pl.BlockSpec((1, tm, d), lambda bi, i: (bi, i, 0)),
        const(w.shape), const(gmat.shape), const(qnw.shape), const(knw.shape),
    ]
    args = [h, w, gmat, qnw, knw]
    if rope:
        in_specs += [pl.BlockSpec((tm, LANES), lambda bi, i: (i, 0))] * 3
        args += list(rope_tabs)
    tok = lambda n: pl.BlockSpec((1, tm, n), lambda bi, i: (bi, i, 0))
    out_shape = [jax.ShapeDtypeStruct((b, l, KV_WIDTH), BF16),
                 jax.ShapeDtypeStruct((b, ATTN_KV_HEADS, V_ROWS, l), BF16)]
    out_specs = [tok(KV_WIDTH),
                 pl.BlockSpec((1, ATTN_KV_HEADS, V_ROWS, tm), lambda bi, i: (bi, 0, 0, i))]
    if need_q:
        out_shape += [jax.ShapeDtypeStruct((b, ATTN_WIDTH, l), BF16),
                      jax.ShapeDtypeStruct((b, l, ATTN_WIDTH), F32)]
        out_specs += [pl.BlockSpec((1, ATTN_WIDTH, tm), lambda bi, i: (bi, 0, i)),
                      tok(ATTN_WIDTH)]
    return pl.pallas_call(
        functools.partial(_attn_in_kernel, rope=rope, need_q=need_q),
        out_shape=out_shape,
        grid=(b, l // tm),
        in_specs=in_specs,
        out_specs=out_specs,
        name="attn_in",
    )(*args)


def _flash_kernel(qt_ref, k_ref, vt_ref, ga_ref, o_ref, *, q_block):
    group = pl.program_id(2) // 2
    k = k_ref[0]
    vt = vt_ref[0, 0]
    def scores(blk, hh):
        qh = qt_ref[0, HEAD_DIM * hh:HEAD_DIM * (hh + 1), pl.ds(blk * q_block, q_block)]
        zero = jnp.zeros_like(qh)
        qpad = jnp.where(group == 0, jnp.concatenate([qh, zero], axis=0),
                         jnp.concatenate([zero, qh], axis=0))
        return _dot(k, qpad)

    def attend(s):
        part = jnp.max(s.reshape(-1, MAX_ROWS, s.shape[1]), axis=0)
        p = jnp.exp2(s - jnp.max(part, axis=0, keepdims=True)).astype(BF16)
        o = _dot(vt, p)
        return o[:HEAD_DIM] / o[HEAD_DIM:HEAD_DIM + 1]

    chains = [(blk, hh) for blk in range(qt_ref.shape[2] // q_block) for hh in range(2)]
    s_next = scores(*chains[0])
    heads = []
    for n, (blk, hh) in enumerate(chains):
        s_cur = s_next
        if n + 1 < len(chains):
            s_next = scores(*chains[n + 1])
        heads.append(attend(s_cur))
        if hh == 1:
            rows = pl.ds(blk * q_block, q_block)
            o2 = jnp.concatenate(heads, axis=0).T
            o_ref[0, rows, :] = o2 * ga_ref[0, rows, :]
            heads = []


def _flash(qt, k, vt, ga):
    b, _, l = qt.shape
    s = k.shape[1]
    tq = min(Q_TILE, l)
    return pl.pallas_call(
        functools.partial(_flash_kernel, q_block=min(Q_BLOCK, tq)),
        out_shape=jax.ShapeDtypeStruct((b, l, ATTN_WIDTH), F32),
        grid=(b, l // tq, ATTN_WIDTH // LANES),
        in_specs=[
            pl.BlockSpec((1, LANES, tq), lambda bi, i, m: (bi, m, i)),
            pl.BlockSpec((1, s, KV_WIDTH), lambda bi, i, m: (bi, 0, 0)),
            pl.BlockSpec((1, 1, V_ROWS, s), lambda bi, i, m: (bi, m // 2, 0, 0)),
            pl.BlockSpec((1, tq, LANES), lambda bi, i, m: (bi, i, m)),
        ],
        out_specs=pl.BlockSpec((1, tq, LANES), lambda bi, i, m: (bi, i, m)),
        name="flash",
    )(qt, k, vt, ga)


def _halo_rows_valid(tile, i, n_tiles):
    rows = tile + 2 * HALO
    r = lax.broadcasted_iota(jnp.int32, (rows, 1), 0)
    head_ok = jnp.where(i > 0, 0, HALO)
    tail_ok = jnp.where(i < n_tiles - 1, rows, HALO + tile)
    return (r >= head_ok) & (r < tail_ok)


def _halo_specs(tm, d, l):
    per = tm // HALO
    last = l // HALO - 1
    prev = pl.BlockSpec((1, HALO, d), lambda bi, i: (bi, jnp.maximum(i * per - 1, 0), 0))
    cur = pl.BlockSpec((1, tm, d), lambda bi, i: (bi, i, 0))
    nxt = pl.BlockSpec((1, HALO, d), lambda bi, i: (bi, jnp.minimum((i + 1) * per, last), 0))
    return [prev, cur, nxt]


def _ssd_in_kernel(hp_ref, h_ref, hn_ref, w_ref, cw_ref, cb_ref, dtb_ref,
                   xbc_ref, dt_ref, z_ref, *, n_tiles):
    i = pl.program_id(1)
    t = h_ref.shape[1]
    he = jnp.concatenate([hp_ref[0], h_ref[0], hn_ref[0]], axis=0)
    conv_ch = xbc_ref.shape[2]
    raw = _dot(he, w_ref[:, :conv_ch])
    raw = jnp.where(_halo_rows_valid(t, i, n_tiles), raw, 0.0)
    acc = jnp.broadcast_to(cb_ref[...], (t, conv_ch))
    rows = t + 2 * HALO
    for k in range(SSD_CONV):
        back = SSD_CONV // 2 - k
        shifted = raw if back == 0 else pltpu.roll(raw, back % rows, 0)
        acc = acc + cw_ref[k:k + 1, :] * shifted[HALO:HALO + t, :]
    xbc_ref[0] = _silu(acc)
    rest = _dot(h_ref[0], w_ref[:, conv_ch:])
    dt_ref[0] = _softplus(rest[:, :2 * LANES] + dtb_ref[...])
    z_ref[0] = rest[:, 2 * LANES:]


def _ssd_in(h, w, conv_w, conv_b, dt_bias):
    b, l, d = h.shape
    tm = min(ROW_TILE, l)
    n_tiles = l // tm
    conv_ch = conv_w.shape[1]
    const = lambda shp: pl.BlockSpec(shp, lambda bi, i: (0,) * len(shp))
    tok = lambda n: pl.BlockSpec((1, tm, n), lambda bi, i: (bi, i, 0))
    return pl.pallas_call(
        functools.partial(_ssd_in_kernel, n_tiles=n_tiles),
        out_shape=[jax.ShapeDtypeStruct((b, l, conv_ch), F32),
                   jax.ShapeDtypeStruct((b, l, 2 * LANES), F32),
                   jax.ShapeDtypeStruct((b, l, SSD_INNER), F32)],
        grid=(b, n_tiles),
        in_specs=_halo_specs(tm, d, l) + [
            const(w.shape), const(conv_w.shape), const((1, conv_ch)), const((1, 2 * LANES)),
        ],
        out_specs=[tok(conv_ch), tok(2 * LANES), tok(SSD_INNER)],
        name="ssd_in",
    )(h, h, h, w, conv_w, conv_b, dt_bias)


_GROUP_LANES = SSD_INNER // SSD_GROUPS


def _lane_block(t, g, w):
    return t[:, w * g:w * (g + 1)]


def _scan_prepare(chunks, expand):
    n = len(chunks)
    q = chunks[0][0].shape[0]
    gw = _GROUP_LANES
    groups = range(SSD_GROUPS)
    gsl = _lane_block
    c_bf = [[gsl(ch[2], g, SSD_STATE).astype(BF16) for g in groups] for ch in chunks]
    b_t = [[gsl(ch[1], g, SSD_STATE).T.astype(BF16) for g in groups] for ch in chunks]
    cb = [[_dot(c_bf[i][g], b_t[i][g]) for g in groups] for i in range(n)]
    a = [ch[3] * ch[4] for ch in chunks]
    cum = [_dot_split_rhs(ch[5], a[i]) for i, ch in enumerate(chunks)]
    wide = [_dot_split_lhs(jnp.concatenate([cum[i], ch[3]], axis=0), expand)
            for i, ch in enumerate(chunks)]
    rows_tot = cum[0].shape[0] - q
    out = []
    lane_lo = lax.broadcasted_iota(jnp.int32, (1, LANES), 1) < SSD_HEAD_DIM
    for i, ch in enumerate(chunks):
        xs, mask = ch[0], ch[6]
        acum = cum[i][:q]
        acum_w = wide[i][:q]
        tot_w = wide[i][q:q + 1]
        xd = xs * wide[i][q + rows_tot:]
        xdd = (xd * jnp.exp(tot_w - acum_w)).astype(BF16)
        xdb = xd.astype(BF16)
        acum_t = acum.T
        upd = [_dot(b_t[i][g], gsl(xdd, g, gw)) for g in groups]
        y_diag = []
        for blk in range(SSD_INNER // LANES):
            g = blk // (gw // LANES)
            xpair = gsl(xdb, blk, LANES)
            res = []
            for hh in range(2):
                head = 2 * blk + hh
                seg = acum[:, head:head + 1] - acum_t[head:head + 1, :]
                lmat = jnp.exp(jnp.where(mask, seg, -jnp.inf))
                res.append(_dot((cb[i][g] * lmat).astype(BF16), xpair))
            y_diag.append(jnp.where(lane_lo, res[0], res[1]))
        out.append((c_bf[i], jnp.concatenate(y_diag, axis=1), jnp.exp(acum_w),
                    jnp.concatenate(upd, axis=1), jnp.exp(tot_w)))
    return out


def _scan_apply(prep, st):
    c_bf, y_diag, decay_out, upd, decay_tot = prep
    st_bf = st.astype(BF16)
    y_off = [_dot(c_bf[g], _lane_block(st_bf, g, _GROUP_LANES)) for g in range(SSD_GROUPS)]
    return y_diag + jnp.concatenate(y_off, axis=1) * decay_out, decay_tot * st + upd


def _ssd_scan_kernel(xf_ref, bf_ref, cf_ref, dtf_ref, xb_ref, bb_ref, cb_ref, dtb_ref,
                     a_ref, tri_ref, mask_ref, e_ref, init_ref, yf_ref, yb_ref, fin_ref, st_ref):
    @pl.when(pl.program_id(1) == 0)
    def _():
        st_ref[...] = init_ref[0]

    q = SSD_CHUNK
    per_step = xf_ref.shape[1] // q
    streams = ((xf_ref, bf_ref, cf_ref, dtf_ref, yf_ref), (xb_ref, bb_ref, cb_ref, dtb_ref, yb_ref))
    order = (list(range(per_step)), list(range(per_step - 1, -1, -1)))
    chunks = []
    for d, (x_ref, b_ref, c_ref, dt_ref, _) in enumerate(streams):
        mask = mask_ref[d] > 0.5
        for j in order[d]:
            rows = pl.ds(j * q, q)
            chunks.append((x_ref[0, rows, :], b_ref[0, rows, :], c_ref[0, rows, :],
                           dt_ref[0, rows, :], a_ref[d], tri_ref[d], mask))
    prep = _scan_prepare(chunks, e_ref[...])
    states = [st_ref[0], st_ref[1]]
    for n in range(per_step):
        for d in range(2):
            y, states[d] = _scan_apply(prep[d * per_step + n], states[d])
            streams[d][4][0, pl.ds(order[d][n] * q, q), :] = y
    for d in range(2):
        st_ref[d] = states[d]
        fin_ref[0, d] = states[d]


def _scan_constants(q):
    r = np.arange(q)
    lower = (r[:, None] >= r[None, :]).astype(np.float32)
    ones = np.ones((SUBLANES, q), np.float32)
    tri = np.stack([np.concatenate([lower, ones], 0), np.concatenate([lower.T, ones], 0)])
    mask = np.stack([lower, lower.T])
    expand = np.zeros((LANES, SSD_INNER), np.float32)
    for hd in range(SSD_HEADS):
        expand[hd, SSD_HEAD_DIM * hd:SSD_HEAD_DIM * (hd + 1)] = 1.0
    return jnp.asarray(tri, BF16), jnp.asarray(mask, F32), jnp.asarray(expand, BF16)


def _ssd_scan(xbc, dt, a_vec, init):
    b, l, _ = xbc.shape
    tri, mask, expand = _scan_constants(SSD_CHUNK)
    q = min(SSD_CHUNK * SCAN_CHUNKS_PER_STEP, l)
    nc = l // q
    bc_w = SSD_GROUPS * SSD_STATE
    bc_blk = SSD_INNER // bc_w
    const = lambda shp: pl.BlockSpec(shp, lambda bi, c: (0,) * len(shp))

    def stream(direction):
        chunk = (lambda c: c) if direction == 0 else (lambda c: nc - 1 - c)
        return [
            pl.BlockSpec((1, q, SSD_INNER), lambda bi, c: (bi, chunk(c), 0)),
            pl.BlockSpec((1, q, bc_w), lambda bi, c: (bi, chunk(c), bc_blk)),
            pl.BlockSpec((1, q, bc_w), lambda bi, c: (bi, chunk(c), bc_blk + 1)),
            pl.BlockSpec((1, q, LANES), lambda bi, c: (bi, chunk(c), direction)),
        ], pl.BlockSpec((1, q, SSD_INNER), lambda bi, c: (bi, chunk(c), 0))

    in_f, out_f = stream(0)
    in_b, out_b = stream(1)
    state_spec = pl.BlockSpec((1, 2, SSD_STATE, SSD_INNER), lambda bi, c: (bi, 0, 0, 0))
    return pl.pallas_call(
        _ssd_scan_kernel,
        out_shape=[jax.ShapeDtypeStruct((b, l, SSD_INNER), F32),
                   jax.ShapeDtypeStruct((b, l, SSD_INNER), F32),
                   jax.ShapeDtypeStruct((b, 2, SSD_STATE, SSD_INNER), F32)],
        grid=(b, nc),
        in_specs=in_f + in_b + [const(a_vec.shape), const(tri.shape), const(mask.shape),
                                const(expand.shape), state_spec],
        out_specs=[out_f, out_b, state_spec],
        scratch_shapes=[pltpu.VMEM((2, SSD_STATE, SSD_INNER), F32)],
        name="ssd_scan",
    )(xbc, xbc, xbc, dt, xbc, xbc, xbc, dt, a_vec, tri, mask, expand, init)


def _conv_kernel(hp_ref, h_ref, hn_ref, w_ref, cw_ref, cb_ref, lnw_ref, lnb_ref, o_ref, *,
                 n_tiles):
    i = pl.program_id(1)
    t = h_ref.shape[1]
    ch = o_ref.shape[2]
    he = jnp.concatenate([hp_ref[0], h_ref[0], hn_ref[0]], axis=0)
    ug = _dot(he, w_ref[:, :2 * ch])
    v = ug[:, :ch] * _sigmoid(ug[:, ch:])
    v = jnp.where(_halo_rows_valid(t, i, n_tiles), v, 0.0)
    first = HALO - CM_KERNEL // 2
    acc = jnp.broadcast_to(cb_ref[...], (t, ch))
    for r in range(SUBLANES):
        part = None
        for al in range(0, 2 * HALO, SUBLANES):
            k = al + r - first
            if 0 <= k < CM_KERNEL:
                term = cw_ref[k:k + 1, :] * v[al:al + t + SUBLANES]
                part = term if part is None else part + term
        acc = acc + part[r:r + t]
    mu = jnp.mean(acc, axis=-1, keepdims=True)
    dev = acc - mu
    var = jnp.mean(dev * dev, axis=-1, keepdims=True)
    y = _silu(dev * lax.rsqrt(var + EPS) * lnw_ref[...] + lnb_ref[...])
    gate = _dot(h_ref[0], w_ref[:, 2 * ch:])
    o_ref[0] = y * _silu(gate)


def _conformer(h, w, conv_w, conv_b, ln_w, ln_b):
    b, l, d = h.shape
    tm = min(ROW_TILE, l)
    n_tiles = l // tm
    ch = conv_w.shape[1]
    const = lambda shp: pl.BlockSpec(shp, lambda bi, i: (0,) * len(shp))
    return pl.pallas_call(
        functools.partial(_conv_kernel, n_tiles=n_tiles),
        out_shape=jax.ShapeDtypeStruct((b, l, ch), F32),
        grid=(b, n_tiles),
        in_specs=_halo_specs(tm, d, l) + [
            const(w.shape), const(conv_w.shape), const((1, ch)), const((1, ch)), const((1, ch)),
        ],
        out_specs=pl.BlockSpec((1, tm, ch), lambda bi, i: (bi, i, 0)),
        name="conformer",
    )(h, h, h, w, conv_w, conv_b, ln_w, ln_b)


def _merge_kernel(*refs, last):
    (x_ref, h_ref, gt_ref, ua_ref, yf_ref, yb_ref, xs_ref, z_ref, uc_ref, wgm_ref, bg_ref,
     wa_ref, ws_ref, wc_ref, wo_ref, dsk_ref, snw_ref, *rest) = refs
    x = x_ref[0]
    d = x.shape[1]
    h = h_ref[0]
    y = yf_ref[0] + yb_ref[0] + dsk_ref[...] * xs_ref[0]
    us = _rms(y * _silu(z_ref[0]), snw_ref[...])
    acc = None
    for j, (u, w_ref) in enumerate(((ua_ref[0], wa_ref), (us, ws_ref), (uc_ref[0], wc_ref))):
        gate = _sigmoid(_dot(h, wgm_ref[:, d * j:d * (j + 1)]) + bg_ref[j:j + 1, :])
        term = gate * _dot(u.astype(BF16), w_ref[...])
        acc = term if acc is None else acc + term
    out = _dot(acc.astype(BF16), wo_ref[...])
    xn = x + gt_ref[0] * out
    if last:
        fnw_ref, o_ref = rest
        o_ref[0] = _rms(xn, fnw_ref[...])
    else:
        sc_ref, sh_ref, nw_ref, o_ref, hn_ref = rest
        o_ref[0] = xn
        hn_ref[0] = _modulated_norm(xn, nw_ref[...], sc_ref[0], sh_ref[0]).astype(BF16)


def _merge(x, h, gate, ua, yf, yb, xbc, z, uc, wgm, bgate, wa, ws, wc, wo, dskip, snw, tail,
           last):
    b, l, d = x.shape
    tm = min(ROW_TILE // 2, l)
    const = lambda shp: pl.BlockSpec(shp, lambda bi, i: (0,) * len(shp))
    tok = lambda n: pl.BlockSpec((1, tm, n), lambda bi, i: (bi, i, 0))
    per_b = pl.BlockSpec((1, 1, d), lambda bi, i: (bi, 0, 0))
    in_specs = [
        tok(d), tok(d), per_b, tok(ATTN_WIDTH), tok(SSD_INNER), tok(SSD_INNER),
        tok(SSD_INNER), tok(SSD_INNER), tok(CM_CH),
        const(wgm.shape), const(bgate.shape), const(wa.shape), const(ws.shape),
        const(wc.shape), const(wo.shape), const((1, SSD_INNER)), const((1, SSD_INNER)),
    ]
    if last:
        in_specs += [const((1, d))]
        out_shape = jax.ShapeDtypeStruct((b, l, d), F32)
        out_specs = tok(d)
    else:
        in_specs += [per_b, per_b, const((1, d))]
        out_shape = [jax.ShapeDtypeStruct((b, l, d), F32), jax.ShapeDtypeStruct((b, l, d), BF16)]
        out_specs = [tok(d), tok(d)]
    return pl.pallas_call(
        functools.partial(_merge_kernel, last=last),
        out_shape=out_shape,
        grid=(b, l // tm),
        in_specs=in_specs,
        out_specs=out_specs,
        name="merge",
    )(x, h, gate, ua, yf, yb, xbc, z, uc, wgm, bgate, wa, ws, wc, wo, dskip, snw, *tail)


def _rope_tables(length):
    rows = length // GRID_W
    row = jnp.repeat(jnp.arange(rows), GRID_W).astype(F32)
    col = jnp.tile(jnp.arange(GRID_W), rows).astype(F32)
    n_freq = HEAD_DIM // 4
    inv = 1.0 / (ROPE_THETA ** (jnp.arange(n_freq, dtype=F32) / n_freq))
    ang_r = row[:, None] * inv
    ang_c = col[:, None] * inv
    zero = jnp.zeros_like(ang_r)
    cos_head = jnp.concatenate([jnp.cos(ang_r)] * 2 + [jnp.cos(ang_c)] * 2, axis=1)
    lo_head = jnp.concatenate([-jnp.sin(ang_r), zero, -jnp.sin(ang_c), zero], axis=1)
    hi_head = jnp.concatenate([zero, jnp.sin(ang_r), zero, jnp.sin(ang_c)], axis=1)
    two = lambda t: jnp.concatenate([t, t], axis=1)
    return two(cos_head), two(lo_head), two(hi_head)


def _head_mean_matrix():
    idx = np.arange(ATTN_WIDTH) // HEAD_DIM
    return jnp.asarray((idx[:, None] == idx[None, :]).astype(np.float32) / HEAD_DIM, BF16)


def _pad_cols(a, width):
    return jnp.pad(a, ((0, 0), (0, width - a.shape[1])))


def kernel(x, c, ctx, c_ctx, w_mod, b_mod, norm_w, w_in, q_norm_w, k_norm_w, ssd_conv_w,
           ssd_conv_b, ssd_A_log, ssd_dt_bias, ssd_D, ssd_norm_w, cm_conv_w, cm_conv_b,
           cm_ln_w, cm_ln_b, w_br_attn, w_br_ssd, w_br_conv, b_gate, w_out, final_norm_w):
    b, l, d = x.shape
    depth = w_mod.shape[0]
    rope_tabs = _rope_tables(l)
    gmat = _head_mean_matrix()

    pad_rows = (-(b + 1)) % SUBLANES
    c_rows = jnp.concatenate([c, c_ctx[None], jnp.zeros((pad_rows, d), F32)], axis=0)
    mod = _modulation(c_rows, w_mod, b_mod)

    def mod_parts(i):
        split = lambda m: (m[:, None, d:2 * d], m[:, None, :d], m[:, None, 2 * d:])
        return split(mod[i, :b]), split(jnp.broadcast_to(mod[i, b:b + 1], (b, 3 * d)))

    o_q, o_k, o_v, o_ga = 0, ATTN_WIDTH, ATTN_WIDTH + KV_WIDTH, ATTN_WIDTH + 2 * KV_WIDTH
    o_xbc = o_ga + ATTN_WIDTH
    conv_ch = SSD_INNER + 2 * SSD_GROUPS * SSD_STATE
    o_dt = o_xbc + conv_ch
    o_z = o_dt + 2 * SSD_HEADS
    o_glu = o_z + SSD_INNER
    o_gcv = o_glu + 2 * CM_CH
    o_gm = o_gcv + CM_CH

    (sc, sh, _), (sc_c, sh_c, _) = mod_parts(0)
    xc = ctx
    h = _hnorm(x, sc, sh, norm_w[0][None])
    hc = _hnorm(xc, sc_c, sh_c, norm_w[0][None])
    for i in range(depth):
        last = i == depth - 1
        wi = w_in[i]
        w_kv = wi[:, o_k:o_ga]
        w_attn = jnp.concatenate([w_kv, wi[:, o_q:o_k], wi[:, o_ga:o_xbc]], axis=1).astype(BF16)
        w_ssd = jnp.concatenate([
            wi[:, o_xbc:o_dt],
            _pad_cols(wi[:, o_dt:o_dt + SSD_HEADS], LANES),
            _pad_cols(wi[:, o_dt + SSD_HEADS:o_z], LANES),
            wi[:, o_z:o_glu]], axis=1).astype(BF16)
        w_cm = wi[:, o_glu:o_gm].astype(BF16)
        w_gm = wi[:, o_gm:].astype(BF16)
        dt_bias = jnp.concatenate([_pad_cols(ssd_dt_bias[i, 0:1], LANES),
                                   _pad_cols(ssd_dt_bias[i, 1:2], LANES)], axis=1)
        a_vec = _pad_cols(-jnp.exp(ssd_A_log[i].astype(F32)), LANES)[:, None, :]
        qnw = jnp.tile(q_norm_w[i], ATTN_HEADS)[None]
        knw = jnp.tile(k_norm_w[i], ATTN_KV_HEADS)[None]
        dskip = jnp.repeat(ssd_D[i], SSD_HEAD_DIM)[None]
        (_, _, gt), (_, _, gt_c) = mod_parts(i)

        if last:
            k_c, vt_c = _attn_in(hc, w_attn[:, :2 * KV_WIDTH], gmat, qnw, knw, None, False)
        else:
            k_c, vt_c, qt_c, ga_c = _attn_in(hc, w_attn, gmat, qnw, knw, None, True)
        k_l, vt_l, qt_l, ga_l = _attn_in(h, w_attn, gmat, qnw, knw, rope_tabs, True)
        ua = _flash(qt_l, jnp.concatenate([k_l, k_c], axis=1),
                    jnp.concatenate([vt_l, vt_c], axis=3), ga_l)

        ssd_args = (w_ssd, ssd_conv_w[i], ssd_conv_b[i][None], dt_bias)
        xbc_c, dt_c, z_c = _ssd_in(hc, *ssd_args)
        zero_state = jnp.zeros((b, 2, SSD_STATE, SSD_INNER), F32)
        yf_c, yb_c, state_c = _ssd_scan(xbc_c, dt_c, a_vec, zero_state)
        xbc_l, dt_l, z_l = _ssd_in(h, *ssd_args)
        yf_l, yb_l, _ = _ssd_scan(xbc_l, dt_l, a_vec, state_c)

        cm_args = (w_cm, cm_conv_w[i], cm_conv_b[i][None], cm_ln_w[i][None], cm_ln_b[i][None])
        uc = _conformer(h, *cm_args)

        merge_w = (w_gm, b_gate[i], w_br_attn[i].astype(BF16), w_br_ssd[i].astype(BF16),
                   w_br_conv[i].astype(BF16), w_out[i].astype(BF16), dskip, ssd_norm_w[i][None])
        if last:
            x = _merge(x, h, gt, ua, yf_l, yb_l, xbc_l, z_l, uc, *merge_w,
                       (final_norm_w[None],), True)
        else:
            (sc, sh, _), (sc_c, sh_c, _) = mod_parts(i + 1)
            nw_next = norm_w[i + 1][None]
            ua_c = _flash(qt_c, k_c, vt_c, ga_c)
            uc_c = _conformer(hc, *cm_args)
            x, h = _merge(x, h, gt, ua, yf_l, yb_l, xbc_l, z_l, uc, *merge_w,
                          (sc, sh, nw_next), False)
            xc, hc = _merge(xc, hc, gt_c, ua_c, yf_c, yb_c, xbc_c, z_c, uc_c, *merge_w,
                            (sc_c, sh_c, nw_next), False)
    return x
```

```python
import functools

import numpy as np
import jax
import jax.numpy as jnp
from jax import lax
from jax.experimental import pallas as pl
from jax.experimental.pallas import tpu as pltpu

F32 = jnp.float32
BF16 = jnp.bfloat16

GRID_W = 64
HEAD_DIM = 64
ATTN_HEADS = 8
ATTN_KV_HEADS = 2
ATTN_WIDTH = ATTN_HEADS * HEAD_DIM
KV_WIDTH = ATTN_KV_HEADS * HEAD_DIM
ROPE_THETA = 10000.0
ATTN_SCALE = HEAD_DIM ** -0.5
SSD_HEADS = 8
SSD_HEAD_DIM = 64
SSD_INNER = SSD_HEADS * SSD_HEAD_DIM
SSD_GROUPS = 2
SSD_STATE = 128
SSD_CONV = 5
SSD_CHUNK = 128
CM_CH = 512
CM_KERNEL = 31
N_BRANCH = 3
EPS = 1e-6
LOG2_E = 1.4426950408889634

LANES = 128
SUBLANES = 8
BF16_ROWS = 16
HALO = BF16_ROWS
ROW_TILE = 512
Q_TILE = 2048
Q_BLOCK = 512
SCAN_CHUNKS_PER_STEP = 4
MAX_ROWS = 256
V_ROWS = HEAD_DIM + BF16_ROWS


def _dot(a, b):
    return jnp.dot(a, b, preferred_element_type=F32)


def _split2(x):
    x1 = x.astype(BF16)
    return x1, (x - x1.astype(F32)).astype(BF16)


def _split3(x):
    x1 = x.astype(BF16)
    r1 = x - x1.astype(F32)
    x2 = r1.astype(BF16)
    r2 = r1 - x2.astype(F32)
    return x1, x2, r2.astype(BF16)


def _dot_split_lhs(x, m):
    x1, x2 = _split2(x)
    return _dot(x1, m) + _dot(x2, m)


def _dot_split_rhs(m, x):
    x1, x2 = _split2(x)
    return _dot(m, x1) + _dot(m, x2)


def _sigmoid(x):
    return 1.0 / (1.0 + jnp.exp(-x))


def _silu(x):
    return x * _sigmoid(x)


def _softplus(x):
    return jnp.maximum(x, 0.0) + jnp.log1p(jnp.exp(-jnp.abs(x)))


def _modulated_norm(x, nw, scale, shift):
    ms = jnp.mean(x * x, axis=-1, keepdims=True)
    return x * lax.rsqrt(ms + EPS) * (nw * (1.0 + scale)) + shift


def _rms(x, w):
    ms = jnp.mean(x * x, axis=-1, keepdims=True)
    return x * lax.rsqrt(ms + EPS) * w


def _mod_kernel(c_ref, w_ref, b_ref, o_ref):
    a1, a2, a3 = _split3(_silu(c_ref[...]))
    w1, w2, w3 = _split3(w_ref[0])
    o = (_dot(a1, w1) + _dot(a1, w2) + _dot(a2, w1)
         + _dot(a2, w2) + _dot(a1, w3) + _dot(a3, w1))
    o_ref[0] = o + b_ref[0]


def _modulation(c_rows, w_mod, b_mod):
    depth, d, d3 = w_mod.shape
    rows = c_rows.shape[0]
    col_tile = 768
    return pl.pallas_call(
        _mod_kernel,
        out_shape=jax.ShapeDtypeStruct((depth, rows, d3), F32),
        grid=(depth, d3 // col_tile),
        in_specs=[
            pl.BlockSpec((rows, d), lambda i, j: (0, 0)),
            pl.BlockSpec((1, d, col_tile), lambda i, j: (i, 0, j)),
            pl.BlockSpec((1, 1, col_tile), lambda i, j: (i, 0, j)),
        ],
        out_specs=pl.BlockSpec((1, rows, col_tile), lambda i, j: (i, 0, j)),
        name="modulation",
    )(c_rows, w_mod, b_mod.reshape(depth, 1, d3))


def _hnorm_kernel(x_ref, sc_ref, sh_ref, nw_ref, h_ref):
    h_ref[0] = _modulated_norm(x_ref[0], nw_ref[...], sc_ref[0], sh_ref[0]).astype(BF16)


def _hnorm(x, scale, shift, nw):
    b, l, d = x.shape
    tm = min(ROW_TILE, l)
    per_b = pl.BlockSpec((1, 1, d), lambda bi, i: (bi, 0, 0))
    tok = pl.BlockSpec((1, tm, d), lambda bi, i: (bi, i, 0))
    return pl.pallas_call(
        _hnorm_kernel,
        out_shape=jax.ShapeDtypeStruct((b, l, d), BF16),
        grid=(b, l // tm),
        in_specs=[tok, per_b, per_b, pl.BlockSpec((1, d), lambda bi, i: (0, 0))],
        out_specs=tok,
        name="hnorm",
    )(x, scale, shift, nw)


def _head_norm(t, gmat, w):
    ms = _dot((t * t).astype(BF16), gmat)
    return t * lax.rsqrt(ms + EPS) * w


def _rope(t, cos, sin_lo, sin_hi):
    outs = []
    for j in range(t.shape[1] // LANES):
        tj = t[:, LANES * j:LANES * (j + 1)]
        outs.append(tj * cos + pltpu.roll(tj, LANES - 16, 1) * sin_lo
                    + pltpu.roll(tj, 16, 1) * sin_hi)
    return outs[0] if len(outs) == 1 else jnp.concatenate(outs, axis=1)


def _attn_in_kernel(*refs, rope, need_q):
    if rope:
        h_ref, w_ref, g_ref, qnw_ref, knw_ref, cos_ref, slo_ref, shi_ref, *outs = refs
    else:
        h_ref, w_ref, g_ref, qnw_ref, knw_ref, *outs = refs
    p = _dot(h_ref[0], w_ref[...])
    k = _head_norm(p[:, :KV_WIDTH], g_ref[:KV_WIDTH, :KV_WIDTH], knw_ref[...])
    if rope:
        k = _rope(k, cos_ref[...], slo_ref[...], shi_ref[...])
    if need_q:
        k_ref, vt_ref, qt_ref, ga_ref = outs
    else:
        k_ref, vt_ref = outs
    k_ref[0] = k.astype(BF16)
    vt = p[:, KV_WIDTH:2 * KV_WIDTH].T
    ones = jnp.ones((V_ROWS - HEAD_DIM, vt.shape[1]), F32)
    for g in range(ATTN_KV_HEADS):
        vt_ref[0, g] = jnp.concatenate([vt[HEAD_DIM * g:HEAD_DIM * (g + 1)], ones],
                                       axis=0).astype(BF16)
    if need_q:
        q0 = 2 * KV_WIDTH
        q = _head_norm(p[:, q0:q0 + ATTN_WIDTH], g_ref[...], qnw_ref[...])
        if rope:
            q = _rope(q, cos_ref[...], slo_ref[...], shi_ref[...])
        qt_ref[0] = (q * (ATTN_SCALE * LOG2_E)).T.astype(BF16)
        ga_ref[0] = _silu(p[:, q0 + ATTN_WIDTH:])


def _attn_in(h, w, gmat, qnw, knw, rope_tabs, need_q):
    b, l, d = h.shape
    tm = min(ROW_TILE, l)
    rope = rope_tabs is not None
    const = lambda shp: pl.BlockSpec(shp, lambda bi, i: (0,) * len(shp))
    in_specs = [
        pl.BlockSpec((1, tm, d), lambda bi, i: (bi, i, 0)),
        const(w.shape), const(gmat.shape), const(qnw.shape), const(knw.shape),
    ]
    args = [h, w, gmat, qnw, knw]
    if rope:
        in_specs += [pl.BlockSpec((tm, LANES), lambda bi, i: (i, 0))] * 3
        args += list(rope_tabs)
    tok = lambda n: pl.BlockSpec((1, tm, n), lambda bi, i: (bi, i, 0))
    out_shape = [jax.ShapeDtypeStruct((b, l, KV_WIDTH), BF16),
                 jax.ShapeDtypeStruct((b, ATTN_KV_HEADS, V_ROWS, l), BF16)]
    out_specs = [tok(KV_WIDTH),
                 pl.BlockSpec((1, ATTN_KV_HEADS, V_ROWS, tm), lambda bi, i: (bi, 0, 0, i))]
    if need_q:
        out_shape += [jax.ShapeDtypeStruct((b, ATTN_WIDTH, l), BF16),
                      jax.ShapeDtypeStruct((b, l, ATTN_WIDTH), F32)]
        out_specs += [pl.BlockSpec((1, ATTN_WIDTH, tm), lambda bi, i: (bi, 0, i)),
                      tok(ATTN_WIDTH)]
    return pl.pallas_call(
        functools.partial(_attn_in_kernel, rope=rope, need_q=need_q),
        out_shape=out_shape,
        grid=(b, l // tm),
        in_specs=in_specs,
        out_specs=out_specs,
        name="attn_in",
    )(*args)


def _flash_kernel(qt_ref, k_ref, vt_ref, ga_ref, o_ref, *, q_block):
    group = pl.program_id(2) // 2
    k = k_ref[0]
    vt = vt_ref[0, 0]
    def scores(blk, hh):
        qh = qt_ref[0, HEAD_DIM * hh:HEAD_DIM * (hh + 1), pl.ds(blk * q_block, q_block)]
        zero = jnp.zeros_like(qh)
        qpad = jnp.where(group == 0, jnp.concatenate([qh, zero], axis=0),
                         jnp.concatenate([zero, qh], axis=0))
        return _dot(k, qpad)

    def attend(s):
        part = jnp.max(s.reshape(-1, MAX_ROWS, s.shape[1]), axis=0)
        p = jnp.exp2(s - jnp.max(part, axis=0, keepdims=True)).astype(BF16)
        o = _dot(vt, p)
        return o[:HEAD_DIM] / o[HEAD_DIM:HEAD_DIM + 1]

    chains = [(blk, hh) for blk in range(qt_ref.shape[2] // q_block) for hh in range(2)]
    s_next = scores(*chains[0])
    heads = []
    for n, (blk, hh) in enumerate(chains):
        s_cur = s_next
        if n + 1 < len(chains):
            s_next = scores(*chains[n + 1])
        heads.append(attend(s_cur))
        if hh == 1:
            rows = pl.ds(blk * q_block, q_block)
            o2 = jnp.concatenate(heads, axis=0).T
            o_ref[0, rows, :] = o2 * ga_ref[0, rows, :]
            heads = []


def _flash(qt, k, vt, ga):
    b, _, l = qt.shape
    s = k.shape[1]
    tq = min(Q_TILE, l)
    return pl.pallas_call(
        functools.partial(_flash_kernel, q_block=min(Q_BLOCK, tq)),
        out_shape=jax.ShapeDtypeStruct((b, l, ATTN_WIDTH), F32),
        grid=(b, l // tq, ATTN_WIDTH // LANES),
        in_specs=[
            pl.BlockSpec((1, LANES, tq), lambda bi, i, m: (bi, m, i)),
            pl.BlockSpec((1, s, KV_WIDTH), lambda bi, i, m: (bi, 0, 0)),
            pl.BlockSpec((1, 1, V_ROWS, s), lambda bi, i, m: (bi, m // 2, 0, 0)),
            pl.BlockSpec((1, tq, LANES), lambda bi, i, m: (bi, i, m)),
        ],
        out_specs=pl.BlockSpec((1, tq, LANES), lambda bi, i, m: (bi, i, m)),
        name="flash",
    )(qt, k, vt, ga)


def _halo_rows_valid(tile, i, n_tiles):
    rows = tile + 2 * HALO
    r = lax.broadcasted_iota(jnp.int32, (rows, 1), 0)
    head_ok = jnp.where(i > 0, 0, HALO)
    tail_ok = jnp.where(i < n_tiles - 1, rows, HALO + tile)
    return (r >= head_ok) & (r < tail_ok)


def _halo_specs(tm, d, l):
    per = tm // HALO
    last = l // HALO - 1
    prev = pl.BlockSpec((1, HALO, d), lambda bi, i: (bi, jnp.maximum(i * per - 1, 0), 0))
    cur = pl.BlockSpec((1, tm, d), lambda bi, i: (bi, i, 0))
    nxt = pl.BlockSpec((1, HALO, d), lambda bi, i: (bi, jnp.minimum((i + 1) * per, last), 0))
    return [prev, cur, nxt]


def _ssd_in_kernel(hp_ref, h_ref, hn_ref, w_ref, cw_ref, cb_ref, dtb_ref,
                   xbc_ref, dt_ref, z_ref, *, n_tiles):
    i = pl.program_id(1)
    t = h_ref.shape[1]
    he = jnp.concatenate([hp_ref[0], h_ref[0], hn_ref[0]], axis=0)
    conv_ch = xbc_ref.shape[2]
    raw = _dot(he, w_ref[:, :conv_ch])
    raw = jnp.where(_halo_rows_valid(t, i, n_tiles), raw, 0.0)
    acc = jnp.broadcast_to(cb_ref[...], (t, conv_ch))
    rows = t + 2 * HALO
    for k in range(SSD_CONV):
        back = SSD_CONV // 2 - k
        shifted = raw if back == 0 else pltpu.roll(raw, back % rows, 0)
        acc = acc + cw_ref[k:k + 1, :] * shifted[HALO:HALO + t, :]
    xbc_ref[0] = _silu(acc)
    rest = _dot(h_ref[0], w_ref[:, conv_ch:])
    dt_ref[0] = _softplus(rest[:, :2 * LANES] + dtb_ref[...])
    z_ref[0] = rest[:, 2 * LANES:]


def _ssd_in(h, w, conv_w, conv_b, dt_bias):
    b, l, d = h.shape
    tm = min(ROW_TILE, l)
    n_tiles = l // tm
    conv_ch = conv_w.shape[1]
    const = lambda shp: pl.BlockSpec(shp, lambda bi, i: (0,) * len(shp))
    tok = lambda n: pl.BlockSpec((1, tm, n), lambda bi, i: (bi, i, 0))
    return pl.pallas_call(
        functools.partial(_ssd_in_kernel, n_tiles=n_tiles),
        out_shape=[jax.ShapeDtypeStruct((b, l, conv_ch), F32),
                   jax.ShapeDtypeStruct((b, l, 2 * LANES), F32),
                   jax.ShapeDtypeStruct((b, l, SSD_INNER), F32)],
        grid=(b, n_tiles),
        in_specs=_halo_specs(tm, d, l) + [
            const(w.shape), const(conv_w.shape), const((1, conv_ch)), const((1, 2 * LANES)),
        ],
        out_specs=[tok(conv_ch), tok(2 * LANES), tok(SSD_INNER)],
        name="ssd_in",
    )(h, h, h, w, conv_w, conv_b, dt_bias)


_GROUP_LANES = SSD_INNER // SSD_GROUPS


def _lane_block(t, g, w):
    return t[:, w * g:w * (g + 1)]


def _scan_prepare(chunks, expand):
    n = len(chunks)
    q = chunks[0][0].shape[0]
    gw = _GROUP_LANES
    groups = range(SSD_GROUPS)
    gsl = _lane_block
    c_bf = [[gsl(ch[2], g, SSD_STATE).astype(BF16) for g in groups] for ch in chunks]
    b_t = [[gsl(ch[1], g, SSD_STATE).T.astype(BF16) for g in groups] for ch in chunks]
    cb = [[_dot(c_bf[i][g], b_t[i][g]) for g in groups] for i in range(n)]
    a = [ch[3] * ch[4] for ch in chunks]
    cum = [_dot_split_rhs(ch[5], a[i]) for i, ch in enumerate(chunks)]
    wide = [_dot_split_lhs(jnp.concatenate([cum[i], ch[3]], axis=0), expand)
            for i, ch in enumerate(chunks)]
    rows_tot = cum[0].shape[0] - q
    out = []
    heads_per_group = SSD_HEADS // SSD_GROUPS
    lane_head = lax.broadcasted_iota(jnp.int32, (1, gw), 1) // SSD_HEAD_DIM
    for i, ch in enumerate(chunks):
        xs, mask = ch[0], ch[6]
        acum = cum[i][:q]
        acum_w = wide[i][:q]
        tot_w = wide[i][q:q + 1]
        xd = xs * wide[i][q + rows_tot:]
        xdd = (xd * jnp.exp(tot_w - acum_w)).astype(BF16)
        acum_t = acum.T
        upd = [_dot(b_t[i][g], gsl(xdd, g, gw)) for g in groups]
        y_diag = []
        for g in groups:
            xg = gsl(xd, g, gw)
            lhs, rhs = [], []
            for j in range(heads_per_group):
                head = heads_per_group * g + j
                seg = acum[:, head:head + 1] - acum_t[head:head + 1, :]
                lmat = jnp.exp(jnp.where(mask, seg, -jnp.inf))
                lhs.append((cb[i][g] * lmat).astype(BF16))
                rhs.append(jnp.where(lane_head == j, xg, 0.0).astype(BF16))
            y_diag.append(_dot(jnp.concatenate(lhs, axis=1), jnp.concatenate(rhs, axis=0)))
        out.append((c_bf[i], jnp.concatenate(y_diag, axis=1), jnp.exp(acum_w),
                    jnp.concatenate(upd, axis=1), jnp.exp(tot_w)))
    return out


def _scan_apply(prep, st):
    c_bf, y_diag, decay_out, upd, decay_tot = prep
    st_bf = st.astype(BF16)
    y_off = [_dot(c_bf[g], _lane_block(st_bf, g, _GROUP_LANES)) for g in range(SSD_GROUPS)]
    return y_diag + jnp.concatenate(y_off, axis=1) * decay_out, decay_tot * st + upd


def _ssd_scan_kernel(xf_ref, bf_ref, cf_ref, dtf_ref, xb_ref, bb_ref, cb_ref, dtb_ref,
                     a_ref, tri_ref, mask_ref, e_ref, init_ref, yf_ref, yb_ref, fin_ref, st_ref):
    @pl.when(pl.program_id(1) == 0)
    def _():
        st_ref[...] = init_ref[0]

    q = SSD_CHUNK
    per_step = xf_ref.shape[1] // q
    streams = ((xf_ref, bf_ref, cf_ref, dtf_ref, yf_ref), (xb_ref, bb_ref, cb_ref, dtb_ref, yb_ref))
    order = (list(range(per_step)), list(range(per_step - 1, -1, -1)))
    chunks = []
    for d, (x_ref, b_ref, c_ref, dt_ref, _) in enumerate(streams):
        mask = mask_ref[d] > 0.5
        for j in order[d]:
            rows = pl.ds(j * q, q)
            chunks.append((x_ref[0, rows, :], b_ref[0, rows, :], c_ref[0, rows, :],
                           dt_ref[0, rows, :], a_ref[d], tri_ref[d], mask))
    prep = _scan_prepare(chunks, e_ref[...])
    states = [st_ref[0], st_ref[1]]
    for n in range(per_step):
        for d in range(2):
            y, states[d] = _scan_apply(prep[d * per_step + n], states[d])
            streams[d][4][0, pl.ds(order[d][n] * q, q), :] = y
    for d in range(2):
        st_ref[d] = states[d]
        fin_ref[0, d] = states[d]


def _scan_constants(q):
    r = np.arange(q)
    lower = (r[:, None] >= r[None, :]).astype(np.float32)
    ones = np.ones((SUBLANES, q), np.float32)
    tri = np.stack([np.concatenate([lower, ones], 0), np.concatenate([lower.T, ones], 0)])
    mask = np.stack([lower, lower.T])
    expand = np.zeros((LANES, SSD_INNER), np.float32)
    for hd in range(SSD_HEADS):
        expand[hd, SSD_HEAD_DIM * hd:SSD_HEAD_DIM * (hd + 1)] = 1.0
    return jnp.asarray(tri, BF16), jnp.asarray(mask, F32), jnp.asarray(expand, BF16)


def _ssd_scan(xbc, dt, a_vec, init):
    b, l, _ = xbc.shape
    tri, mask, expand = _scan_constants(SSD_CHUNK)
    q = min(SSD_CHUNK * SCAN_CHUNKS_PER_STEP, l)
    nc = l // q
    bc_w = SSD_GROUPS * SSD_STATE
    bc_blk = SSD_INNER // bc_w
    const = lambda shp: pl.BlockSpec(shp, lambda bi, c: (0,) * len(shp))

    def stream(direction):
        chunk = (lambda c: c) if direction == 0 else (lambda c: nc - 1 - c)
        return [
            pl.BlockSpec((1, q, SSD_INNER), lambda bi, c: (bi, chunk(c), 0)),
            pl.BlockSpec((1, q, bc_w), lambda bi, c: (bi, chunk(c), bc_blk)),
            pl.BlockSpec((1, q, bc_w), lambda bi, c: (bi, chunk(c), bc_blk + 1)),
            pl.BlockSpec((1, q, LANES), lambda bi, c: (bi, chunk(c), direction)),
        ], pl.BlockSpec((1, q, SSD_INNER), lambda bi, c: (bi, chunk(c), 0))

    in_f, out_f = stream(0)
    in_b, out_b = stream(1)
    state_spec = pl.BlockSpec((1, 2, SSD_STATE, SSD_INNER), lambda bi, c: (bi, 0, 0, 0))
    return pl.pallas_call(
        _ssd_scan_kernel,
        out_shape=[jax.ShapeDtypeStruct((b, l, SSD_INNER), F32),
                   jax.ShapeDtypeStruct((b, l, SSD_INNER), F32),
                   jax.ShapeDtypeStruct((b, 2, SSD_STATE, SSD_INNER), F32)],
        grid=(b, nc),
        in_specs=in_f + in_b + [const(a_vec.shape), const(tri.shape), const(mask.shape),
                                const(expand.shape), state_spec],
        out_specs=[out_f, out_b, state_spec],
        scratch_shapes=[pltpu.VMEM((2, SSD_STATE, SSD_INNER), F32)],
        name="ssd_scan",
    )(xbc, xbc, xbc, dt, xbc, xbc, xbc, dt, a_vec, tri, mask, expand, init)


def _conv_residue(v, cw_ref, r, t):
    first = HALO - CM_KERNEL // 2
    part = None
    for al in range(0, 2 * HALO, SUBLANES):
        k = al + r - first
        if 0 <= k < CM_KERNEL:
            term = cw_ref[k:k + 1, :] * v[al:al + t + SUBLANES]
            part = term if part is None else part + term
    return part[r:r + t]


def _conformer_tail(acc, glu_gate, lnw_ref, lnb_ref):
    mu = jnp.mean(acc, axis=-1, keepdims=True)
    dev = acc - mu
    var = jnp.mean(dev * dev, axis=-1, keepdims=True)
    y = _silu(dev * lax.rsqrt(var + EPS) * lnw_ref[...] + lnb_ref[...])
    return y * _silu(glu_gate)


def _merge_kernel(*refs, last, n_tiles):
    (x_ref, hp_ref, h_ref, hn_ref, gt_ref, ua_ref, yf_ref, yb_ref, xs_ref, z_ref, wgm_ref, bg_ref,
     wa_ref, ws_ref, wc_ref, wo_ref, dsk_ref, snw_ref, wcm_ref, cw_ref, cb_ref, lnw_ref, lnb_ref,
     *rest) = refs
    x = x_ref[0]
    t, d = x.shape
    h = h_ref[0]
    he = jnp.concatenate([hp_ref[0], h, hn_ref[0]], axis=0)
    ug = _dot(he, wcm_ref[:, :2 * CM_CH])
    glu_gate = _dot(h, wcm_ref[:, 2 * CM_CH:])
    v = ug[:, :CM_CH] * _sigmoid(ug[:, CM_CH:])
    v = jnp.where(_halo_rows_valid(t, pl.program_id(1), n_tiles), v, 0.0)
    y = yf_ref[0] + yb_ref[0] + dsk_ref[...] * xs_ref[0]
    ua = ua_ref[0].astype(BF16)
    us = _rms(y * _silu(z_ref[0]), snw_ref[...]).astype(BF16)

    def gate_block(j, cols):
        return _sigmoid(_dot(h, wgm_ref[:, d * j + cols.start:d * j + cols.stop])
                        + bg_ref[j:j + 1, cols])

    n_blocks = SUBLANES // 2
    width = d // n_blocks
    conv = jnp.broadcast_to(cb_ref[...], (t, CM_CH))
    partial, conv_gate = [], []
    for r in range(SUBLANES):
        conv = conv + _conv_residue(v, cw_ref, r, t)
        if r % 2 == 1:
            cols = slice((r // 2) * width, (r // 2 + 1) * width)
            partial.append(gate_block(0, cols) * _dot(ua, wa_ref[:, cols])
                           + gate_block(1, cols) * _dot(us, ws_ref[:, cols]))
            conv_gate.append(gate_block(2, cols))
    uc = _conformer_tail(conv, glu_gate, lnw_ref, lnb_ref)
    acc = (jnp.concatenate(partial, axis=1)
           + jnp.concatenate(conv_gate, axis=1) * _dot(uc.astype(BF16), wc_ref[...]))
    out = _dot(acc.astype(BF16), wo_ref[...])
    xn = x + gt_ref[0] * out
    if last:
        fnw_ref, o_ref = rest
        o_ref[0] = _rms(xn, fnw_ref[...])
    else:
        sc_ref, sh_ref, nw_ref, o_ref, hn_ref = rest
        o_ref[0] = xn
        hn_ref[0] = _modulated_norm(xn, nw_ref[...], sc_ref[0], sh_ref[0]).astype(BF16)


def _merge(x, h, gate, ua, yf, yb, xbc, z, wgm, bgate, wa, ws, wc, wo, dskip, snw,
           wcm, cm_conv_w, cm_conv_b, cm_ln_w, cm_ln_b, tail, last):
    b, l, d = x.shape
    tm = min(ROW_TILE // 2, l)
    const = lambda shp: pl.BlockSpec(shp, lambda bi, i: (0,) * len(shp))
    tok = lambda n: pl.BlockSpec((1, tm, n), lambda bi, i: (bi, i, 0))
    per_b = pl.BlockSpec((1, 1, d), lambda bi, i: (bi, 0, 0))
    in_specs = [tok(d)] + _halo_specs(tm, d, l) + [
        per_b, tok(ATTN_WIDTH), tok(SSD_INNER), tok(SSD_INNER), tok(SSD_INNER), tok(SSD_INNER),
        const(wgm.shape), const(bgate.shape), const(wa.shape), const(ws.shape),
        const(wc.shape), const(wo.shape), const((1, SSD_INNER)), const((1, SSD_INNER)),
        const(wcm.shape), const(cm_conv_w.shape), const((1, CM_CH)), const((1, CM_CH)),
        const((1, CM_CH)),
    ]
    if last:
        in_specs += [const((1, d))]
        out_shape = jax.ShapeDtypeStruct((b, l, d), F32)
        out_specs = tok(d)
    else:
        in_specs += [per_b, per_b, const((1, d))]
        out_shape = [jax.ShapeDtypeStruct((b, l, d), F32), jax.ShapeDtypeStruct((b, l, d), BF16)]
        out_specs = [tok(d), tok(d)]
    return pl.pallas_call(
        functools.partial(_merge_kernel, last=last, n_tiles=l // tm),
        out_shape=out_shape,
        grid=(b, l // tm),
        in_specs=in_specs,
        out_specs=out_specs,
        name="merge",
    )(x, h, h, h, gate, ua, yf, yb, xbc, z, wgm, bgate, wa, ws, wc, wo, dskip, snw,
      wcm, cm_conv_w, cm_conv_b, cm_ln_w, cm_ln_b, *tail)


def _rope_tables(length):
    rows = length // GRID_W
    row = jnp.repeat(jnp.arange(rows), GRID_W).astype(F32)
    col = jnp.tile(jnp.arange(GRID_W), rows).astype(F32)
    n_freq = HEAD_DIM // 4
    inv = 1.0 / (ROPE_THETA ** (jnp.arange(n_freq, dtype=F32) / n_freq))
    ang_r = row[:, None] * inv
    ang_c = col[:, None] * inv
    zero = jnp.zeros_like(ang_r)
    cos_head = jnp.concatenate([jnp.cos(ang_r)] * 2 + [jnp.cos(ang_c)] * 2, axis=1)
    lo_head = jnp.concatenate([-jnp.sin(ang_r), zero, -jnp.sin(ang_c), zero], axis=1)
    hi_head = jnp.concatenate([zero, jnp.sin(ang_r), zero, jnp.sin(ang_c)], axis=1)
    two = lambda t: jnp.concatenate([t, t], axis=1)
    return two(cos_head), two(lo_head), two(hi_head)


def _head_mean_matrix():
    idx = np.arange(ATTN_WIDTH) // HEAD_DIM
    return jnp.asarray((idx[:, None] == idx[None, :]).astype(np.float32) / HEAD_DIM, BF16)


def _pad_cols(a, width):
    return jnp.pad(a, ((0, 0), (0, width - a.shape[1])))


def kernel(x, c, ctx, c_ctx, w_mod, b_mod, norm_w, w_in, q_norm_w, k_norm_w, ssd_conv_w,
           ssd_conv_b, ssd_A_log, ssd_dt_bias, ssd_D, ssd_norm_w, cm_conv_w, cm_conv_b,
           cm_ln_w, cm_ln_b, w_br_attn, w_br_ssd, w_br_conv, b_gate, w_out, final_norm_w):
    b, l, d = x.shape
    depth = w_mod.shape[0]
    rope_tabs = _rope_tables(l)
    gmat = _head_mean_matrix()

    pad_rows = (-(b + 1)) % SUBLANES
    c_rows = jnp.concatenate([c, c_ctx[None], jnp.zeros((pad_rows, d), F32)], axis=0)
    mod = _modulation(c_rows, w_mod, b_mod)

    def mod_parts(i):
        split = lambda m: (m[:, None, d:2 * d], m[:, None, :d], m[:, None, 2 * d:])
        return split(mod[i, :b]), split(jnp.broadcast_to(mod[i, b:b + 1], (b, 3 * d)))

    o_q, o_k, o_v, o_ga = 0, ATTN_WIDTH, ATTN_WIDTH + KV_WIDTH, ATTN_WIDTH + 2 * KV_WIDTH
    o_xbc = o_ga + ATTN_WIDTH
    conv_ch = SSD_INNER + 2 * SSD_GROUPS * SSD_STATE
    o_dt = o_xbc + conv_ch
    o_z = o_dt + 2 * SSD_HEADS
    o_glu = o_z + SSD_INNER
    o_gcv = o_glu + 2 * CM_CH
    o_gm = o_gcv + CM_CH

    (sc, sh, _), (sc_c, sh_c, _) = mod_parts(0)
    xc = ctx
    h = _hnorm(x, sc, sh, norm_w[0][None])
    hc = _hnorm(xc, sc_c, sh_c, norm_w[0][None])
    for i in range(depth):
        last = i == depth - 1
        wi = w_in[i]
        w_kv = wi[:, o_k:o_ga]
        w_attn = jnp.concatenate([w_kv, wi[:, o_q:o_k], wi[:, o_ga:o_xbc]], axis=1).astype(BF16)
        w_ssd = jnp.concatenate([
            wi[:, o_xbc:o_dt],
            _pad_cols(wi[:, o_dt:o_dt + SSD_HEADS], LANES),
            _pad_cols(wi[:, o_dt + SSD_HEADS:o_z], LANES),
            wi[:, o_z:o_glu]], axis=1).astype(BF16)
        w_cm = wi[:, o_glu:o_gm].astype(BF16)
        w_gm = wi[:, o_gm:].astype(BF16)
        dt_bias = jnp.concatenate([_pad_cols(ssd_dt_bias[i, 0:1], LANES),
                                   _pad_cols(ssd_dt_bias[i, 1:2], LANES)], axis=1)
        a_vec = _pad_cols(-jnp.exp(ssd_A_log[i].astype(F32)), LANES)[:, None, :]
        qnw = jnp.tile(q_norm_w[i], ATTN_HEADS)[None]
        knw = jnp.tile(k_norm_w[i], ATTN_KV_HEADS)[None]
        dskip = jnp.repeat(ssd_D[i], SSD_HEAD_DIM)[None]
        (_, _, gt), (_, _, gt_c) = mod_parts(i)

        if last:
            k_c, vt_c = _attn_in(hc, w_attn[:, :2 * KV_WIDTH], gmat, qnw, knw, None, False)
        else:
            k_c, vt_c, qt_c, ga_c = _attn_in(hc, w_attn, gmat, qnw, knw, None, True)
        k_l, vt_l, qt_l, ga_l = _attn_in(h, w_attn, gmat, qnw, knw, rope_tabs, True)
        ua = _flash(qt_l, jnp.concatenate([k_l, k_c], axis=1),
                    jnp.concatenate([vt_l, vt_c], axis=3), ga_l)

        ssd_args = (w_ssd, ssd_conv_w[i], ssd_conv_b[i][None], dt_bias)
        xbc_c, dt_c, z_c = _ssd_in(hc, *ssd_args)
        zero_state = jnp.zeros((b, 2, SSD_STATE, SSD_INNER), F32)
        yf_c, yb_c, state_c = _ssd_scan(xbc_c, dt_c, a_vec, zero_state)
        xbc_l, dt_l, z_l = _ssd_in(h, *ssd_args)
        yf_l, yb_l, _ = _ssd_scan(xbc_l, dt_l, a_vec, state_c)

        merge_w = (w_gm, b_gate[i], w_br_attn[i].astype(BF16), w_br_ssd[i].astype(BF16),
                   w_br_conv[i].astype(BF16), w_out[i].astype(BF16), dskip, ssd_norm_w[i][None],
                   w_cm, cm_conv_w[i], cm_conv_b[i][None], cm_ln_w[i][None], cm_ln_b[i][None])
        if last:
            x = _merge(x, h, gt, ua, yf_l, yb_l, xbc_l, z_l, *merge_w,
                       (final_norm_w[None],), True)
        else:
            (sc, sh, _), (sc_c, sh_c, _) = mod_parts(i + 1)
            nw_next = norm_w[i + 1][None]
            ua_c = _flash(qt_c, k_c, vt_c, ga_c)
            x, h = _merge(x, h, gt, ua, yf_l, yb_l, xbc_l, z_l, *merge_w,
                          (sc, sh, nw_next), False)
            xc, hc = _merge(xc, hc, gt_c, ua_c, yf_c, yb_c, xbc_c, z_c, *merge_w,
                            (sc_c, sh_c, nw_next), False)
    return x
```

```python
import functools

import numpy as np
import jax
import jax.numpy as jnp
from jax import lax
from jax.experimental import pallas as pl
from jax.experimental.pallas import tpu as pltpu

F32 = jnp.float32
BF16 = jnp.bfloat16

GRID_W = 64
HEAD_DIM = 64
ATTN_HEADS = 8
ATTN_KV_HEADS = 2
ATTN_WIDTH = ATTN_HEADS * HEAD_DIM
KV_WIDTH = ATTN_KV_HEADS * HEAD_DIM
ROPE_THETA = 10000.0
ATTN_SCALE = HEAD_DIM ** -0.5
SSD_HEADS = 8
SSD_HEAD_DIM = 64
SSD_INNER = SSD_HEADS * SSD_HEAD_DIM
SSD_GROUPS = 2
SSD_STATE = 128
SSD_CONV = 5
SSD_CHUNK = 128
CM_CH = 512
CM_KERNEL = 31
N_BRANCH = 3
EPS = 1e-6
LOG2_E = 1.4426950408889634

LANES = 128
SUBLANES = 8
BF16_ROWS = 16
HALO = BF16_ROWS
ROW_TILE = 512
Q_TILE = 2048
Q_BLOCK = 512
SCAN_CHUNKS_PER_STEP = 4
MAX_SAFE_SCORE_BOUND = 48.0
MAX_ROWS = 256
V_ROWS = HEAD_DIM + BF16_ROWS


def _dot(a, b):
    return jnp.dot(a, b, preferred_element_type=F32)


def _split2(x):
    x1 = x.astype(BF16)
    return x1, (x - x1.astype(F32)).astype(BF16)


def _split3(x):
    x1 = x.astype(BF16)
    r1 = x - x1.astype(F32)
    x2 = r1.astype(BF16)
    r2 = r1 - x2.astype(F32)
    return x1, x2, r2.astype(BF16)


def _dot_split_lhs(x, m):
    x1, x2 = _split2(x)
    return _dot(x1, m) + _dot(x2, m)


def _dot_split_rhs(m, x):
    x1, x2 = _split2(x)
    return _dot(m, x1) + _dot(m, x2)


def _sigmoid(x):
    return 1.0 / (1.0 + jnp.exp(-x))


def _silu(x):
    return x * _sigmoid(x)


def _softplus(x):
    return jnp.maximum(x, 0.0) + jnp.log1p(jnp.exp(-jnp.abs(x)))


def _modulated_norm(x, nw, scale, shift):
    ms = jnp.mean(x * x, axis=-1, keepdims=True)
    return x * lax.rsqrt(ms + EPS) * (nw * (1.0 + scale)) + shift


def _rms(x, w):
    ms = jnp.mean(x * x, axis=-1, keepdims=True)
    return x * lax.rsqrt(ms + EPS) * w


def _mod_kernel(c_ref, w_ref, b_ref, o_ref):
    a1, a2, a3 = _split3(_silu(c_ref[...]))
    w1, w2, w3 = _split3(w_ref[0])
    o = (_dot(a1, w1) + _dot(a1, w2) + _dot(a2, w1)
         + _dot(a2, w2) + _dot(a1, w3) + _dot(a3, w1))
    o_ref[0] = o + b_ref[0]


def _modulation(c_rows, w_mod, b_mod):
    depth, d, d3 = w_mod.shape
    rows = c_rows.shape[0]
    col_tile = 768
    return pl.pallas_call(
        _mod_kernel,
        out_shape=jax.ShapeDtypeStruct((depth, rows, d3), F32),
        grid=(depth, d3 // col_tile),
        in_specs=[
            pl.BlockSpec((rows, d), lambda i, j: (0, 0)),
            pl.BlockSpec((1, d, col_tile), lambda i, j: (i, 0, j)),
            pl.BlockSpec((1, 1, col_tile), lambda i, j: (i, 0, j)),
        ],
        out_specs=pl.BlockSpec((1, rows, col_tile), lambda i, j: (i, 0, j)),
        name="modulation",
    )(c_rows, w_mod, b_mod.reshape(depth, 1, d3))


def _hnorm_kernel(x_ref, sc_ref, sh_ref, nw_ref, h_ref):
    h_ref[0] = _modulated_norm(x_ref[0], nw_ref[...], sc_ref[0], sh_ref[0]).astype(BF16)


def _hnorm(x, scale, shift, nw):
    b, l, d = x.shape
    tm = min(ROW_TILE, l)
    per_b = pl.BlockSpec((1, 1, d), lambda bi, i: (bi, 0, 0))
    tok = pl.BlockSpec((1, tm, d), lambda bi, i: (bi, i, 0))
    return pl.pallas_call(
        _hnorm_kernel,
        out_shape=jax.ShapeDtypeStruct((b, l, d), BF16),
        grid=(b, l // tm),
        in_specs=[tok, per_b, per_b, pl.BlockSpec((1, d), lambda bi, i: (0, 0))],
        out_specs=tok,
        name="hnorm",
    )(x, scale, shift, nw)


def _head_norm(t, gmat, w):
    ms = _dot((t * t).astype(BF16), gmat)
    return t * lax.rsqrt(ms + EPS) * w


def _rope(t, cos, sin_lo, sin_hi):
    outs = []
    for j in range(t.shape[1] // LANES):
        tj = t[:, LANES * j:LANES * (j + 1)]
        outs.append(tj * cos + pltpu.roll(tj, LANES - 16, 1) * sin_lo
                    + pltpu.roll(tj, 16, 1) * sin_hi)
    return outs[0] if len(outs) == 1 else jnp.concatenate(outs, axis=1)


def _attn_in_kernel(*refs, rope, need_q):
    if rope:
        h_ref, w_ref, g_ref, qnw_ref, knw_ref, cos_ref, slo_ref, shi_ref, *outs = refs
    else:
        h_ref, w_ref, g_ref, qnw_ref, knw_ref, *outs = refs
    p = _dot(h_ref[0], w_ref[...])
    k = _head_norm(p[:, :KV_WIDTH], g_ref[:KV_WIDTH, :KV_WIDTH], knw_ref[...])
    if rope:
        k = _rope(k, cos_ref[...], slo_ref[...], shi_ref[...])
    if need_q:
        k_ref, vt_ref, qt_ref, ga_ref = outs
    else:
        k_ref, vt_ref = outs
    k_ref[0] = k.astype(BF16)
    vt = p[:, KV_WIDTH:2 * KV_WIDTH].T
    ones = jnp.ones((V_ROWS - HEAD_DIM, vt.shape[1]), F32)
    for g in range(ATTN_KV_HEADS):
        vt_ref[0, g] = jnp.concatenate([vt[HEAD_DIM * g:HEAD_DIM * (g + 1)], ones],
                                       axis=0).astype(BF16)
    if need_q:
        q0 = 2 * KV_WIDTH
        q = _head_norm(p[:, q0:q0 + ATTN_WIDTH], g_ref[...], qnw_ref[...])
        if rope:
            q = _rope(q, cos_ref[...], slo_ref[...], shi_ref[...])
        qt_ref[0] = (q * (ATTN_SCALE * LOG2_E)).T.astype(BF16)
        ga_ref[0] = _silu(p[:, q0 + ATTN_WIDTH:])


def _attn_in(h, w, gmat, qnw, knw, rope_tabs, need_q):
    b, l, d = h.shape
    tm = min(ROW_TILE, l)
    rope = rope_tabs is not None
    const = lambda shp: pl.BlockSpec(shp, lambda bi, i: (0,) * len(shp))
    in_specs = [
        pl.BlockSpec((1, tm, d), lambda bi, i: (bi, i, 0)),
        const(w.shape), const(gmat.shape), const(qnw.shape), const(knw.shape),
    ]
    args = [h, w, gmat, qnw, knw]
    if rope:
        in_specs += [pl.BlockSpec((tm, LANES), lambda bi, i: (i, 0))] * 3
        args += list(rope_tabs)
    tok = lambda n: pl.BlockSpec((1, tm, n), lambda bi, i: (bi, i, 0))
    out_shape = [jax.ShapeDtypeStruct((b, l, KV_WIDTH), BF16),
                 jax.ShapeDtypeStruct((b, ATTN_KV_HEADS, V_ROWS, l), BF16)]
    out_specs = [tok(KV_WIDTH),
                 pl.BlockSpec((1, ATTN_KV_HEADS, V_ROWS, tm), lambda bi, i: (bi, 0, 0, i))]
    if need_q:
        out_shape += [jax.ShapeDtypeStruct((b, ATTN_WIDTH, l), BF16),
                      jax.ShapeDtypeStruct((b, l, ATTN_WIDTH), F32)]
        out_specs += [pl.BlockSpec((1, ATTN_WIDTH, tm), lambda bi, i: (bi, 0, i)),
                      tok(ATTN_WIDTH)]
    return pl.pallas_call(
        functools.partial(_attn_in_kernel, rope=rope, need_q=need_q),
        out_shape=out_shape,
        grid=(b, l // tm),
        in_specs=in_specs,
        out_specs=out_specs,
        name="attn_in",
    )(*args)


def _flash_kernel(*refs, q_block, bounded):
    if bounded:
        bound_ref, qt_ref, k_ref, vt_ref, ga_ref, o_ref = refs
    else:
        qt_ref, k_ref, vt_ref, ga_ref, o_ref = refs
    group = pl.program_id(2) // 2
    k = k_ref[0]
    vt = vt_ref[0, 0]

    def scores(blk, hh):
        qh = qt_ref[0, HEAD_DIM * hh:HEAD_DIM * (hh + 1), pl.ds(blk * q_block, q_block)]
        zero = jnp.zeros_like(qh)
        qpad = jnp.where(group == 0, jnp.concatenate([qh, zero], axis=0),
                         jnp.concatenate([zero, qh], axis=0))
        return _dot(k, qpad)

    def attend(s):
        if bounded:
            shift = bound_ref[...]
        else:
            part = jnp.max(s.reshape(-1, MAX_ROWS, s.shape[1]), axis=0)
            shift = jnp.max(part, axis=0, keepdims=True)
        p = jnp.exp2(s - shift).astype(BF16)
        o = _dot(vt, p)
        return o[:HEAD_DIM] / o[HEAD_DIM:HEAD_DIM + 1]

    chains = [(blk, hh) for blk in range(qt_ref.shape[2] // q_block) for hh in range(2)]
    s_next = scores(*chains[0])
    heads = []
    for n, (blk, hh) in enumerate(chains):
        s_cur = s_next
        if n + 1 < len(chains):
            s_next = scores(*chains[n + 1])
        heads.append(attend(s_cur))
        if hh == 1:
            rows = pl.ds(blk * q_block, q_block)
            o2 = jnp.concatenate(heads, axis=0).T
            o_ref[0, rows, :] = o2 * ga_ref[0, rows, :]
            heads = []


def _flash_call(qt, k, vt, ga, bound):
    b, _, l = qt.shape
    s = k.shape[1]
    tq = min(Q_TILE, l)
    q_block = min(Q_BLOCK, tq)
    in_specs = [
        pl.BlockSpec((1, LANES, tq), lambda bi, i, m: (bi, m, i)),
        pl.BlockSpec((1, s, KV_WIDTH), lambda bi, i, m: (bi, 0, 0)),
        pl.BlockSpec((1, 1, V_ROWS, s), lambda bi, i, m: (bi, m // 2, 0, 0)),
        pl.BlockSpec((1, tq, LANES), lambda bi, i, m: (bi, i, m)),
    ]
    args = [qt, k, vt, ga]
    if bound is not None:
        in_specs.insert(0, pl.BlockSpec((1, q_block), lambda bi, i, m: (0, 0)))
        args.insert(0, jnp.full((1, q_block), bound, F32))
    return pl.pallas_call(
        functools.partial(_flash_kernel, q_block=q_block, bounded=bound is not None),
        out_shape=jax.ShapeDtypeStruct((b, l, ATTN_WIDTH), F32),
        grid=(b, l // tq, ATTN_WIDTH // LANES),
        in_specs=in_specs,
        out_specs=pl.BlockSpec((1, tq, LANES), lambda bi, i, m: (bi, i, m)),
        name="flash_bounded" if bound is not None else "flash",
    )(*args)


def _flash(qt, k, vt, ga, score_bound):
    return lax.cond(score_bound <= MAX_SAFE_SCORE_BOUND,
                    lambda: _flash_call(qt, k, vt, ga, score_bound),
                    lambda: _flash_call(qt, k, vt, ga, None))


def _score_bound(q_norm_w, k_norm_w):
    return (HEAD_DIM * ATTN_SCALE * LOG2_E
            * jnp.max(jnp.abs(q_norm_w)) * jnp.max(jnp.abs(k_norm_w)))


def _halo_rows_valid(tile, i, n_tiles):
    rows = tile + 2 * HALO
    r = lax.broadcasted_iota(jnp.int32, (rows, 1), 0)
    head_ok = jnp.where(i > 0, 0, HALO)
    tail_ok = jnp.where(i < n_tiles - 1, rows, HALO + tile)
    return (r >= head_ok) & (r < tail_ok)


def _halo_specs(tm, d, l):
    per = tm // HALO
    last = l // HALO - 1
    prev = pl.BlockSpec((1, HALO, d), lambda bi, i: (bi, jnp.maximum(i * per - 1, 0), 0))
    cur = pl.BlockSpec((1, tm, d), lambda bi, i: (bi, i, 0))
    nxt = pl.BlockSpec((1, HALO, d), lambda bi, i: (bi, jnp.minimum((i + 1) * per, last), 0))
    return [prev, cur, nxt]


def _ssd_in_kernel(hp_ref, h_ref, hn_ref, w_ref, cw_ref, cb_ref, dtb_ref,
                   xbc_ref, dt_ref, z_ref, *, n_tiles):
    i = pl.program_id(1)
    t = h_ref.shape[1]
    he = jnp.concatenate([hp_ref[0], h_ref[0], hn_ref[0]], axis=0)
    conv_ch = xbc_ref.shape[2]
    raw = _dot(he, w_ref[:, :conv_ch])
    raw = jnp.where(_halo_rows_valid(t, i, n_tiles), raw, 0.0)
    acc = jnp.broadcast_to(cb_ref[...], (t, conv_ch))
    rows = t + 2 * HALO
    for k in range(SSD_CONV):
        back = SSD_CONV // 2 - k
        shifted = raw if back == 0 else pltpu.roll(raw, back % rows, 0)
        acc = acc + cw_ref[k:k + 1, :] * shifted[HALO:HALO + t, :]
    xbc_ref[0] = _silu(acc)
    rest = _dot(h_ref[0], w_ref[:, conv_ch:])
    dt_ref[0] = _softplus(rest[:, :2 * LANES] + dtb_ref[...])
    z_ref[0] = rest[:, 2 * LANES:]


def _ssd_in(h, w, conv_w, conv_b, dt_bias):
    b, l, d = h.shape
    tm = min(ROW_TILE, l)
    n_tiles = l // tm
    conv_ch = conv_w.shape[1]
    const = lambda shp: pl.BlockSpec(shp, lambda bi, i: (0,) * len(shp))
    tok = lambda n: pl.BlockSpec((1, tm, n), lambda bi, i: (bi, i, 0))
    return pl.pallas_call(
        functools.partial(_ssd_in_kernel, n_tiles=n_tiles),
        out_shape=[jax.ShapeDtypeStruct((b, l, conv_ch), F32),
                   jax.ShapeDtypeStruct((b, l, 2 * LANES), F32),
                   jax.ShapeDtypeStruct((b, l, SSD_INNER), F32)],
        grid=(b, n_tiles),
        in_specs=_halo_specs(tm, d, l) + [
            const(w.shape), const(conv_w.shape), const((1, conv_ch)), const((1, 2 * LANES)),
        ],
        out_specs=[tok(conv_ch), tok(2 * LANES), tok(SSD_INNER)],
        name="ssd_in",
    )(h, h, h, w, conv_w, conv_b, dt_bias)


_GROUP_LANES = SSD_INNER // SSD_GROUPS


def _lane_block(t, g, w):
    return t[:, w * g:w * (g + 1)]


def _scan_prepare(chunks, expand):
    n = len(chunks)
    q = chunks[0][0].shape[0]
    gw = _GROUP_LANES
    groups = range(SSD_GROUPS)
    gsl = _lane_block
    c_bf = [[gsl(ch[2], g, SSD_STATE).astype(BF16) for g in groups] for ch in chunks]
    b_t = [[gsl(ch[1], g, SSD_STATE).T.astype(BF16) for g in groups] for ch in chunks]
    cb = [[_dot(c_bf[i][g], b_t[i][g]) for g in groups] for i in range(n)]
    a = [ch[3] * ch[4] for ch in chunks]
    cum = [_dot_split_rhs(ch[5], a[i]) for i, ch in enumerate(chunks)]
    wide = [_dot_split_lhs(jnp.concatenate([cum[i], ch[3]], axis=0), expand)
            for i, ch in enumerate(chunks)]
    rows_tot = cum[0].shape[0] - q
    out = []
    heads_per_group = SSD_HEADS // SSD_GROUPS
    lane_head = lax.broadcasted_iota(jnp.int32, (1, gw), 1) // SSD_HEAD_DIM
    for i, ch in enumerate(chunks):
        xs, mask = ch[0], ch[6]
        acum = cum[i][:q]
        acum_w = wide[i][:q]
        tot_w = wide[i][q:q + 1]
        xd = xs * wide[i][q + rows_tot:]
        xdd = (xd * jnp.exp(tot_w - acum_w)).astype(BF16)
        acum_t = acum.T
        upd = [_dot(b_t[i][g], gsl(xdd, g, gw)) for g in groups]
        y_diag = []
        for g in groups:
            xg = gsl(xd, g, gw)
            lhs, rhs = [], []
            for j in range(heads_per_group):
                head = heads_per_group * g + j
                seg = acum[:, head:head + 1] - acum_t[head:head + 1, :]
                lmat = jnp.exp(jnp.where(mask, seg, -jnp.inf))
                lhs.append((cb[i][g] * lmat).astype(BF16))
                rhs.append(jnp.where(lane_head == j, xg, 0.0).astype(BF16))
            y_diag.append(_dot(jnp.concatenate(lhs, axis=1), jnp.concatenate(rhs, axis=0)))
        out.append((c_bf[i], jnp.concatenate(y_diag, axis=1), jnp.exp(acum_w),
                    jnp.concatenate(upd, axis=1), jnp.exp(tot_w)))
    return out


def _scan_apply(prep, st):
    c_bf, y_diag, decay_out, upd, decay_tot = prep
    st_bf = st.astype(BF16)
    y_off = [_dot(c_bf[g], _lane_block(st_bf, g, _GROUP_LANES)) for g in range(SSD_GROUPS)]
    return y_diag + jnp.concatenate(y_off, axis=1) * decay_out, decay_tot * st + upd


def _ssd_scan_kernel(xf_ref, bf_ref, cf_ref, dtf_ref, xb_ref, bb_ref, cb_ref, dtb_ref,
                     a_ref, tri_ref, mask_ref, e_ref, init_ref, yf_ref, yb_ref, fin_ref, st_ref):
    @pl.when(pl.program_id(1) == 0)
    def _():
        st_ref[...] = init_ref[0]

    q = SSD_CHUNK
    per_step = xf_ref.shape[1] // q
    streams = ((xf_ref, bf_ref, cf_ref, dtf_ref, yf_ref), (xb_ref, bb_ref, cb_ref, dtb_ref, yb_ref))
    order = (list(range(per_step)), list(range(per_step - 1, -1, -1)))
    chunks = []
    for d, (x_ref, b_ref, c_ref, dt_ref, _) in enumerate(streams):
        mask = mask_ref[d] > 0.5
        for j in order[d]:
            rows = pl.ds(j * q, q)
            chunks.append((x_ref[0, rows, :], b_ref[0, rows, :], c_ref[0, rows, :],
                           dt_ref[0, rows, :], a_ref[d], tri_ref[d], mask))
    prep = _scan_prepare(chunks, e_ref[...])
    states = [st_ref[0], st_ref[1]]
    for n in range(per_step):
        for d in range(2):
            y, states[d] = _scan_apply(prep[d * per_step + n], states[d])
            streams[d][4][0, pl.ds(order[d][n] * q, q), :] = y
    for d in range(2):
        st_ref[d] = states[d]
        fin_ref[0, d] = states[d]


def _scan_constants(q):
    r = np.arange(q)
    lower = (r[:, None] >= r[None, :]).astype(np.float32)
    ones = np.ones((SUBLANES, q), np.float32)
    tri = np.stack([np.concatenate([lower, ones], 0), np.concatenate([lower.T, ones], 0)])
    mask = np.stack([lower, lower.T])
    expand = np.zeros((LANES, SSD_INNER), np.float32)
    for hd in range(SSD_HEADS):
        expand[hd, SSD_HEAD_DIM * hd:SSD_HEAD_DIM * (hd + 1)] = 1.0
    return jnp.asarray(tri, BF16), jnp.asarray(mask, F32), jnp.asarray(expand, BF16)


def _ssd_scan(xbc, dt, a_vec, init):
    b, l, _ = xbc.shape
    tri, mask, expand = _scan_constants(SSD_CHUNK)
    q = min(SSD_CHUNK * SCAN_CHUNKS_PER_STEP, l)
    nc = l // q
    bc_w = SSD_GROUPS * SSD_STATE
    bc_blk = SSD_INNER // bc_w
    const = lambda shp: pl.BlockSpec(shp, lambda bi, c: (0,) * len(shp))

    def stream(direction):
        chunk = (lambda c: c) if direction == 0 else (lambda c: nc - 1 - c)
        return [
            pl.BlockSpec((1, q, SSD_INNER), lambda bi, c: (bi, chunk(c), 0)),
            pl.BlockSpec((1, q, bc_w), lambda bi, c: (bi, chunk(c), bc_blk)),
            pl.BlockSpec((1, q, bc_w), lambda bi, c: (bi, chunk(c), bc_blk + 1)),
            pl.BlockSpec((1, q, LANES), lambda bi, c: (bi, chunk(c), direction)),
        ], pl.BlockSpec((1, q, SSD_INNER), lambda bi, c: (bi, chunk(c), 0))

    in_f, out_f = stream(0)
    in_b, out_b = stream(1)
    state_spec = pl.BlockSpec((1, 2, SSD_STATE, SSD_INNER), lambda bi, c: (bi, 0, 0, 0))
    return pl.pallas_call(
        _ssd_scan_kernel,
        out_shape=[jax.ShapeDtypeStruct((b, l, SSD_INNER), F32),
                   jax.ShapeDtypeStruct((b, l, SSD_INNER), F32),
                   jax.ShapeDtypeStruct((b, 2, SSD_STATE, SSD_INNER), F32)],
        grid=(b, nc),
        in_specs=in_f + in_b + [const(a_vec.shape), const(tri.shape), const(mask.shape),
                                const(expand.shape), state_spec],
        out_specs=[out_f, out_b, state_spec],
        scratch_shapes=[pltpu.VMEM((2, SSD_STATE, SSD_INNER), F32)],
        name="ssd_scan",
    )(xbc, xbc, xbc, dt, xbc, xbc, xbc, dt, a_vec, tri, mask, expand, init)


def _conv_residue(v, cw_ref, r, t):
    first = HALO - CM_KERNEL // 2
    part = None
    for al in range(0, 2 * HALO, SUBLANES):
        k = al + r - first
        if 0 <= k < CM_KERNEL:
            term = cw_ref[k:k + 1, :] * v[al:al + t + SUBLANES]
            part = term if part is None else part + term
    return part[r:r + t]


def _conformer_tail(acc, glu_gate, lnw_ref, lnb_ref):
    mu = jnp.mean(acc, axis=-1, keepdims=True)
    dev = acc - mu
    var = jnp.mean(dev * dev, axis=-1, keepdims=True)
    y = _silu(dev * lax.rsqrt(var + EPS) * lnw_ref[...] + lnb_ref[...])
    return y * _silu(glu_gate)


def _merge_kernel(*refs, last, n_tiles):
    (x_ref, hp_ref, h_ref, hn_ref, gt_ref, ua_ref, yf_ref, yb_ref, xs_ref, z_ref, wgm_ref, bg_ref,
     wa_ref, ws_ref, wc_ref, wo_ref, dsk_ref, snw_ref, wcm_ref, cw_ref, cb_ref, lnw_ref, lnb_ref,
     *rest) = refs
    x = x_ref[0]
    t, d = x.shape
    h = h_ref[0]
    he = jnp.concatenate([hp_ref[0], h, hn_ref[0]], axis=0)
    ug = _dot(he, wcm_ref[:, :2 * CM_CH])
    glu_gate = _dot(h, wcm_ref[:, 2 * CM_CH:])
    v = ug[:, :CM_CH] * _sigmoid(ug[:, CM_CH:])
    v = jnp.where(_halo_rows_valid(t, pl.program_id(1), n_tiles), v, 0.0)
    y = yf_ref[0] + yb_ref[0] + dsk_ref[...] * xs_ref[0]
    ua = ua_ref[0].astype(BF16)
    us = _rms(y * _silu(z_ref[0]), snw_ref[...]).astype(BF16)

    def gate_block(j, cols):
        return _sigmoid(_dot(h, wgm_ref[:, d * j + cols.start:d * j + cols.stop])
                        + bg_ref[j:j + 1, cols])

    n_blocks = SUBLANES // 2
    width = d // n_blocks
    conv = jnp.broadcast_to(cb_ref[...], (t, CM_CH))
    partial, conv_gate = [], []
    for r in range(SUBLANES):
        conv = conv + _conv_residue(v, cw_ref, r, t)
        if r % 2 == 1:
            cols = slice((r // 2) * width, (r // 2 + 1) * width)
            partial.append(gate_block(0, cols) * _dot(ua, wa_ref[:, cols])
                           + gate_block(1, cols) * _dot(us, ws_ref[:, cols]))
            conv_gate.append(gate_block(2, cols))
    uc = _conformer_tail(conv, glu_gate, lnw_ref, lnb_ref)
    acc = (jnp.concatenate(partial, axis=1)
           + jnp.concatenate(conv_gate, axis=1) * _dot(uc.astype(BF16), wc_ref[...]))
    out = _dot(acc.astype(BF16), wo_ref[...])
    xn = x + gt_ref[0] * out
    if last:
        fnw_ref, o_ref = rest
        o_ref[0] = _rms(xn, fnw_ref[...])
    else:
        sc_ref, sh_ref, nw_ref, o_ref, hn_ref = rest
        o_ref[0] = xn
        hn_ref[0] = _modulated_norm(xn, nw_ref[...], sc_ref[0], sh_ref[0]).astype(BF16)


def _merge(x, h, gate, ua, yf, yb, xbc, z, wgm, bgate, wa, ws, wc, wo, dskip, snw,
           wcm, cm_conv_w, cm_conv_b, cm_ln_w, cm_ln_b, tail, last):
    b, l, d = x.shape
    tm = min(ROW_TILE // 2, l)
    const = lambda shp: pl.BlockSpec(shp, lambda bi, i: (0,) * len(shp))
    tok = lambda n: pl.BlockSpec((1, tm, n), lambda bi, i: (bi, i, 0))
    per_b = pl.BlockSpec((1, 1, d), lambda bi, i: (bi, 0, 0))
    in_specs = [tok(d)] + _halo_specs(tm, d, l) + [
        per_b, tok(ATTN_WIDTH), tok(SSD_INNER), tok(SSD_INNER), tok(SSD_INNER), tok(SSD_INNER),
        const(wgm.shape), const(bgate.shape), const(wa.shape), const(ws.shape),
        const(wc.shape), const(wo.shape), const((1, SSD_INNER)), const((1, SSD_INNER)),
        const(wcm.shape), const(cm_conv_w.shape), const((1, CM_CH)), const((1, CM_CH)),
        const((1, CM_CH)),
    ]
    if last:
        in_specs += [const((1, d))]
        out_shape = jax.ShapeDtypeStruct((b, l, d), F32)
        out_specs = tok(d)
    else:
        in_specs += [per_b, per_b, const((1, d))]
        out_shape = [jax.ShapeDtypeStruct((b, l, d), F32), jax.ShapeDtypeStruct((b, l, d), BF16)]
        out_specs = [tok(d), tok(d)]
    return pl.pallas_call(
        functools.partial(_merge_kernel, last=last, n_tiles=l // tm),
        out_shape=out_shape,
        grid=(b, l // tm),
        in_specs=in_specs,
        out_specs=out_specs,
        name="merge",
    )(x, h, h, h, gate, ua, yf, yb, xbc, z, wgm, bgate, wa, ws, wc, wo, dskip, snw,
      wcm, cm_conv_w, cm_conv_b, cm_ln_w, cm_ln_b, *tail)


def _rope_tables(length):
    rows = length // GRID_W
    row = jnp.repeat(jnp.arange(rows), GRID_W).astype(F32)
    col = jnp.tile(jnp.arange(GRID_W), rows).astype(F32)
    n_freq = HEAD_DIM // 4
    inv = 1.0 / (ROPE_THETA ** (jnp.arange(n_freq, dtype=F32) / n_freq))
    ang_r = row[:, None] * inv
    ang_c = col[:, None] * inv
    zero = jnp.zeros_like(ang_r)
    cos_head = jnp.concatenate([jnp.cos(ang_r)] * 2 + [jnp.cos(ang_c)] * 2, axis=1)
    lo_head = jnp.concatenate([-jnp.sin(ang_r), zero, -jnp.sin(ang_c), zero], axis=1)
    hi_head = jnp.concatenate([zero, jnp.sin(ang_r), zero, jnp.sin(ang_c)], axis=1)
    two = lambda t: jnp.concatenate([t, t], axis=1)
    return two(cos_head), two(lo_head), two(hi_head)


def _head_mean_matrix():
    idx = np.arange(ATTN_WIDTH) // HEAD_DIM
    return jnp.asarray((idx[:, None] == idx[None, :]).astype(np.float32) / HEAD_DIM, BF16)


def _pad_cols(a, width):
    return jnp.pad(a, ((0, 0), (0, width - a.shape[1])))


def kernel(x, c, ctx, c_ctx, w_mod, b_mod, norm_w, w_in, q_norm_w, k_norm_w, ssd_conv_w,
           ssd_conv_b, ssd_A_log, ssd_dt_bias, ssd_D, ssd_norm_w, cm_conv_w, cm_conv_b,
           cm_ln_w, cm_ln_b, w_br_attn, w_br_ssd, w_br_conv, b_gate, w_out, final_norm_w):
    b, l, d = x.shape
    depth = w_mod.shape[0]
    rope_tabs = _rope_tables(l)
    gmat = _head_mean_matrix()

    pad_rows = (-(b + 1)) % SUBLANES
    c_rows = jnp.concatenate([c, c_ctx[None], jnp.zeros((pad_rows, d), F32)], axis=0)
    mod = _modulation(c_rows, w_mod, b_mod)

    def mod_parts(i):
        split = lambda m: (m[:, None, d:2 * d], m[:, None, :d], m[:, None, 2 * d:])
        return split(mod[i, :b]), split(jnp.broadcast_to(mod[i, b:b + 1], (b, 3 * d)))

    o_q, o_k, o_v, o_ga = 0, ATTN_WIDTH, ATTN_WIDTH + KV_WIDTH, ATTN_WIDTH + 2 * KV_WIDTH
    o_xbc = o_ga + ATTN_WIDTH
    conv_ch = SSD_INNER + 2 * SSD_GROUPS * SSD_STATE
    o_dt = o_xbc + conv_ch
    o_z = o_dt + 2 * SSD_HEADS
    o_glu = o_z + SSD_INNER
    o_gcv = o_glu + 2 * CM_CH
    o_gm = o_gcv + CM_CH

    (sc, sh, _), (sc_c, sh_c, _) = mod_parts(0)
    xc = ctx
    h = _hnorm(x, sc, sh, norm_w[0][None])
    hc = _hnorm(xc, sc_c, sh_c, norm_w[0][None])
    for i in range(depth):
        last = i == depth - 1
        wi = w_in[i]
        w_kv = wi[:, o_k:o_ga]
        w_attn = jnp.concatenate([w_kv, wi[:, o_q:o_k], wi[:, o_ga:o_xbc]], axis=1).astype(BF16)
        w_ssd = jnp.concatenate([
            wi[:, o_xbc:o_dt],
            _pad_cols(wi[:, o_dt:o_dt + SSD_HEADS], LANES),
            _pad_cols(wi[:, o_dt + SSD_HEADS:o_z], LANES),
            wi[:, o_z:o_glu]], axis=1).astype(BF16)
        w_cm = wi[:, o_glu:o_gm].astype(BF16)
        w_gm = wi[:, o_gm:].astype(BF16)
        dt_bias = jnp.concatenate([_pad_cols(ssd_dt_bias[i, 0:1], LANES),
                                   _pad_cols(ssd_dt_bias[i, 1:2], LANES)], axis=1)
        a_vec = _pad_cols(-jnp.exp(ssd_A_log[i].astype(F32)), LANES)[:, None, :]
        qnw = jnp.tile(q_norm_w[i], ATTN_HEADS)[None]
        knw = jnp.tile(k_norm_w[i], ATTN_KV_HEADS)[None]
        dskip = jnp.repeat(ssd_D[i], SSD_HEAD_DIM)[None]
        (_, _, gt), (_, _, gt_c) = mod_parts(i)

        if last:
            k_c, vt_c = _attn_in(hc, w_attn[:, :2 * KV_WIDTH], gmat, qnw, knw, None, False)
        else:
            k_c, vt_c, qt_c, ga_c = _attn_in(hc, w_attn, gmat, qnw, knw, None, True)
        k_l, vt_l, qt_l, ga_l = _attn_in(h, w_attn, gmat, qnw, knw, rope_tabs, True)
        score_bound = _score_bound(q_norm_w[i], k_norm_w[i])
        ua = _flash(qt_l, jnp.concatenate([k_l, k_c], axis=1),
                    jnp.concatenate([vt_l, vt_c], axis=3), ga_l, score_bound)

        ssd_args = (w_ssd, ssd_conv_w[i], ssd_conv_b[i][None], dt_bias)
        xbc_c, dt_c, z_c = _ssd_in(hc, *ssd_args)
        zero_state = jnp.zeros((b, 2, SSD_STATE, SSD_INNER), F32)
        yf_c, yb_c, state_c = _ssd_scan(xbc_c, dt_c, a_vec, zero_state)
        xbc_l, dt_l, z_l = _ssd_in(h, *ssd_args)
        yf_l, yb_l, _ = _ssd_scan(xbc_l, dt_l, a_vec, state_c)

        merge_w = (w_gm, b_gate[i], w_br_attn[i].astype(BF16), w_br_ssd[i].astype(BF16),
                   w_br_conv[i].astype(BF16), w_out[i].astype(BF16), dskip, ssd_norm_w[i][None],
                   w_cm, cm_conv_w[i], cm_conv_b[i][None], cm_ln_w[i][None], cm_ln_b[i][None])
        if last:
            x = _merge(x, h, gt, ua, yf_l, yb_l, xbc_l, z_l, *merge_w,
                       (final_norm_w[None],), True)
        else:
            (sc, sh, _), (sc_c, sh_c, _) = mod_parts(i + 1)
            nw_next = norm_w[i + 1][None]
            ua_c = _flash(qt_c, k_c, vt_c, ga_c, score_bound)
            x, h = _merge(x, h, gt, ua, yf_l, yb_l, xbc_l, z_l, *merge_w,
                          (sc, sh, nw_next), False)
            xc, hc = _merge(xc, hc, gt_c, ua_c, yf_c, yb_c, xbc_c, z_c, *merge_w,
                            (sc_c, sh_c, nw_next), False)
    return x
```

```python
import functools

import numpy as np
import jax
import jax.numpy as jnp
from jax import lax
from jax.experimental import pallas as pl
from jax.experimental.pallas import tpu as pltpu

F32 = jnp.float32
BF16 = jnp.bfloat16

GRID_W = 64
HEAD_DIM = 64
ATTN_HEADS = 8
ATTN_KV_HEADS = 2
ATTN_WIDTH = ATTN_HEADS * HEAD_DIM
KV_WIDTH = ATTN_KV_HEADS * HEAD_DIM
ROPE_THETA = 10000.0
ATTN_SCALE = HEAD_DIM ** -0.5
SSD_HEADS = 8
SSD_HEAD_DIM = 64
SSD_INNER = SSD_HEADS * SSD_HEAD_DIM
SSD_GROUPS = 2
SSD_STATE = 128
SSD_CONV = 5
SSD_CHUNK = 128
CM_CH = 512
CM_KERNEL = 31
N_BRANCH = 3
EPS = 1e-6
LOG2_E = 1.4426950408889634

LANES = 128
SUBLANES = 8
BF16_ROWS = 16
HALO = BF16_ROWS
ROW_TILE = 1024
MERGE_TILE = 512
Q_TILE = 2048
Q_BLOCK = 512
SCAN_CHUNKS_PER_STEP = 4
MAX_SAFE_SCORE_BOUND = 48.0
MAX_ROWS = 256
V_ROWS = HEAD_DIM + BF16_ROWS


def _dot(a, b):
    return jnp.dot(a, b, preferred_element_type=F32)


def _split2(x):
    x1 = x.astype(BF16)
    return x1, (x - x1.astype(F32)).astype(BF16)


def _split3(x):
    x1 = x.astype(BF16)
    r1 = x - x1.astype(F32)
    x2 = r1.astype(BF16)
    r2 = r1 - x2.astype(F32)
    return x1, x2, r2.astype(BF16)


def _dot_split_lhs(x, m):
    x1, x2 = _split2(x)
    return _dot(x1, m) + _dot(x2, m)


def _dot_split_rhs(m, x):
    x1, x2 = _split2(x)
    return _dot(m, x1) + _dot(m, x2)


def _sigmoid(x):
    return 1.0 / (1.0 + jnp.exp(-x))


def _silu(x):
    return x * _sigmoid(x)


def _softplus(x):
    return jnp.maximum(x, 0.0) + jnp.log1p(jnp.exp(-jnp.abs(x)))


def _modulated_norm(x, nw, scale, shift):
    ms = jnp.mean(x * x, axis=-1, keepdims=True)
    return x * lax.rsqrt(ms + EPS) * (nw * (1.0 + scale)) + shift


def _rms(x, w):
    ms = jnp.mean(x * x, axis=-1, keepdims=True)
    return x * lax.rsqrt(ms + EPS) * w


def _mod_kernel(c_ref, w_ref, b_ref, o_ref):
    a1, a2, a3 = _split3(_silu(c_ref[...]))
    w1, w2, w3 = _split3(w_ref[0])
    o = (_dot(a1, w1) + _dot(a1, w2) + _dot(a2, w1)
         + _dot(a2, w2) + _dot(a1, w3) + _dot(a3, w1))
    o_ref[0] = o + b_ref[0]


def _modulation(c_rows, w_mod, b_mod):
    depth, d, d3 = w_mod.shape
    rows = c_rows.shape[0]
    col_tile = 768
    return pl.pallas_call(
        _mod_kernel,
        out_shape=jax.ShapeDtypeStruct((depth, rows, d3), F32),
        grid=(depth, d3 // col_tile),
        in_specs=[
            pl.BlockSpec((rows, d), lambda i, j: (0, 0)),
            pl.BlockSpec((1, d, col_tile), lambda i, j: (i, 0, j)),
            pl.BlockSpec((1, 1, col_tile), lambda i, j: (i, 0, j)),
        ],
        out_specs=pl.BlockSpec((1, rows, col_tile), lambda i, j: (i, 0, j)),
        name="modulation",
    )(c_rows, w_mod, b_mod.reshape(depth, 1, d3))


def _hnorm_kernel(x_ref, sc_ref, sh_ref, nw_ref, h_ref):
    h_ref[0] = _modulated_norm(x_ref[0], nw_ref[...], sc_ref[0], sh_ref[0]).astype(BF16)


def _hnorm(x, scale, shift, nw):
    b, l, d = x.shape
    tm = min(ROW_TILE, l)
    per_b = pl.BlockSpec((1, 1, d), lambda bi, i: (bi, 0, 0))
    tok = pl.BlockSpec((1, tm, d), lambda bi, i: (bi, i, 0))
    return pl.pallas_call(
        _hnorm_kernel,
        out_shape=jax.ShapeDtypeStruct((b, l, d), BF16),
        grid=(b, l // tm),
        in_specs=[tok, per_b, per_b, pl.BlockSpec((1, d), lambda bi, i: (0, 0))],
        out_specs=tok,
        name="hnorm",
    )(x, scale, shift, nw)


def _head_norm(t, gmat, w):
    ms = _dot((t * t).astype(BF16), gmat)
    return t * lax.rsqrt(ms + EPS) * w


def _rope(t, cos, sin_lo, sin_hi):
    outs = []
    for j in range(t.shape[1] // LANES):
        tj = t[:, LANES * j:LANES * (j + 1)]
        outs.append(tj * cos + pltpu.roll(tj, LANES - 16, 1) * sin_lo
                    + pltpu.roll(tj, 16, 1) * sin_hi)
    return outs[0] if len(outs) == 1 else jnp.concatenate(outs, axis=1)


def _attn_in_kernel(*refs, rope, need_q):
    if rope:
        h_ref, w_ref, g_ref, qnw_ref, knw_ref, cos_ref, slo_ref, shi_ref, *outs = refs
    else:
        h_ref, w_ref, g_ref, qnw_ref, knw_ref, *outs = refs
    p = _dot(h_ref[0], w_ref[...])
    k = _head_norm(p[:, :KV_WIDTH], g_ref[:KV_WIDTH, :KV_WIDTH], knw_ref[...])
    if rope:
        k = _rope(k, cos_ref[...], slo_ref[...], shi_ref[...])
    if need_q:
        k_ref, vt_ref, qt_ref, ga_ref = outs
    else:
        k_ref, vt_ref = outs
    k_ref[0] = k.astype(BF16)
    vt = p[:, KV_WIDTH:2 * KV_WIDTH].T
    ones = jnp.ones((V_ROWS - HEAD_DIM, vt.shape[1]), F32)
    for g in range(ATTN_KV_HEADS):
        vt_ref[0, g] = jnp.concatenate([vt[HEAD_DIM * g:HEAD_DIM * (g + 1)], ones],
                                       axis=0).astype(BF16)
    if need_q:
        q0 = 2 * KV_WIDTH
        q = _head_norm(p[:, q0:q0 + ATTN_WIDTH], g_ref[...], qnw_ref[...])
        if rope:
            q = _rope(q, cos_ref[...], slo_ref[...], shi_ref[...])
        qt_ref[0] = (q * (ATTN_SCALE * LOG2_E)).T.astype(BF16)
        ga_ref[0] = _silu(p[:, q0 + ATTN_WIDTH:])


def _attn_in(h, w, gmat, qnw, knw, rope_tabs, need_q):
    b, l, d = h.shape
    tm = min(ROW_TILE, l)
    rope = rope_tabs is not None
    const = lambda shp: pl.BlockSpec(shp, lambda bi, i: (0,) * len(shp))
    in_specs = [
        pl.BlockSpec((1, tm, d), lambda bi, i: (bi, i, 0)),
        const(w.shape), const(gmat.shape), const(qnw.shape), const(knw.shape),
    ]
    args = [h, w, gmat, qnw, knw]
    if rope:
        in_specs += [pl.BlockSpec((tm, LANES), lambda bi, i: (i, 0))] * 3
        args += list(rope_tabs)
    tok = lambda n: pl.BlockSpec((1, tm, n), lambda bi, i: (bi, i, 0))
    out_shape = [jax.ShapeDtypeStruct((b, l, KV_WIDTH), BF16),
                 jax.ShapeDtypeStruct((b, ATTN_KV_HEADS, V_ROWS, l), BF16)]
    out_specs = [tok(KV_WIDTH),
                 pl.BlockSpec((1, ATTN_KV_HEADS, V_ROWS, tm), lambda bi, i: (bi, 0, 0, i))]
    if need_q:
        out_shape += [jax.ShapeDtypeStruct((b, ATTN_WIDTH, l), BF16),
                      jax.ShapeDtypeStruct((b, l, ATTN_WIDTH), F32)]
        out_specs += [pl.BlockSpec((1, ATTN_WIDTH, tm), lambda bi, i: (bi, 0, i)),
                      tok(ATTN_WIDTH)]
    return pl.pallas_call(
        functools.partial(_attn_in_kernel, rope=rope, need_q=need_q),
        out_shape=out_shape,
        grid=(b, l // tm),
        in_specs=in_specs,
        out_specs=out_specs,
        name="attn_in",
    )(*args)


def _flash_kernel(*refs, q_block, bounded):
    if bounded:
        bound_ref, qt_ref, k_ref, vt_ref, ga_ref, o_ref = refs
    else:
        qt_ref, k_ref, vt_ref, ga_ref, o_ref = refs
    group = pl.program_id(2) // 2
    k = k_ref[0]
    vt = vt_ref[0, 0]

    def scores(blk, hh):
        qh = qt_ref[0, HEAD_DIM * hh:HEAD_DIM * (hh + 1), pl.ds(blk * q_block, q_block)]
        zero = jnp.zeros_like(qh)
        qpad = jnp.where(group == 0, jnp.concatenate([qh, zero], axis=0),
                         jnp.concatenate([zero, qh], axis=0))
        return _dot(k, qpad)

    def attend(s):
        if bounded:
            shift = bound_ref[...]
        else:
            part = jnp.max(s.reshape(-1, MAX_ROWS, s.shape[1]), axis=0)
            shift = jnp.max(part, axis=0, keepdims=True)
        p = jnp.exp2(s - shift).astype(BF16)
        o = _dot(vt, p)
        return o[:HEAD_DIM] / o[HEAD_DIM:HEAD_DIM + 1]

    chains = [(blk, hh) for blk in range(qt_ref.shape[2] // q_block) for hh in range(2)]
    s_next = scores(*chains[0])
    heads = []
    for n, (blk, hh) in enumerate(chains):
        s_cur = s_next
        if n + 1 < len(chains):
            s_next = scores(*chains[n + 1])
        heads.append(attend(s_cur))
        if hh == 1:
            rows = pl.ds(blk * q_block, q_block)
            o2 = jnp.concatenate(heads, axis=0).T
            o_ref[0, rows, :] = o2 * ga_ref[0, rows, :]
            heads = []


def _flash_call(qt, k, vt, ga, bound):
    b, _, l = qt.shape
    s = k.shape[1]
    tq = min(Q_TILE, l)
    q_block = min(Q_BLOCK, tq)
    in_specs = [
        pl.BlockSpec((1, LANES, tq), lambda bi, i, m: (bi, m, i)),
        pl.BlockSpec((1, s, KV_WIDTH), lambda bi, i, m: (bi, 0, 0)),
        pl.BlockSpec((1, 1, V_ROWS, s), lambda bi, i, m: (bi, m // 2, 0, 0)),
        pl.BlockSpec((1, tq, LANES), lambda bi, i, m: (bi, i, m)),
    ]
    args = [qt, k, vt, ga]
    if bound is not None:
        in_specs.insert(0, pl.BlockSpec((1, q_block), lambda bi, i, m: (0, 0)))
        args.insert(0, jnp.full((1, q_block), bound, F32))
    return pl.pallas_call(
        functools.partial(_flash_kernel, q_block=q_block, bounded=bound is not None),
        out_shape=jax.ShapeDtypeStruct((b, l, ATTN_WIDTH), F32),
        grid=(b, l // tq, ATTN_WIDTH // LANES),
        in_specs=in_specs,
        out_specs=pl.BlockSpec((1, tq, LANES), lambda bi, i, m: (bi, i, m)),
        name="flash_bounded" if bound is not None else "flash",
    )(*args)


def _flash(qt, k, vt, ga, score_bound):
    return lax.cond(score_bound <= MAX_SAFE_SCORE_BOUND,
                    lambda: _flash_call(qt, k, vt, ga, score_bound),
                    lambda: _flash_call(qt, k, vt, ga, None))


def _score_bound(q_norm_w, k_norm_w):
    return (HEAD_DIM * ATTN_SCALE * LOG2_E
            * jnp.max(jnp.abs(q_norm_w)) * jnp.max(jnp.abs(k_norm_w)))


def _halo_rows_valid(tile, i, n_tiles):
    rows = tile + 2 * HALO
    r = lax.broadcasted_iota(jnp.int32, (rows, 1), 0)
    head_ok = jnp.where(i > 0, 0, HALO)
    tail_ok = jnp.where(i < n_tiles - 1, rows, HALO + tile)
    return (r >= head_ok) & (r < tail_ok)


def _halo_specs(tm, d, l):
    per = tm // HALO
    last = l // HALO - 1
    prev = pl.BlockSpec((1, HALO, d), lambda bi, i: (bi, jnp.maximum(i * per - 1, 0), 0))
    cur = pl.BlockSpec((1, tm, d), lambda bi, i: (bi, i, 0))
    nxt = pl.BlockSpec((1, HALO, d), lambda bi, i: (bi, jnp.minimum((i + 1) * per, last), 0))
    return [prev, cur, nxt]


def _ssd_in_kernel(hp_ref, h_ref, hn_ref, w_ref, cw_ref, cb_ref, dtb_ref,
                   xbc_ref, dt_ref, z_ref, *, n_tiles):
    i = pl.program_id(1)
    t = h_ref.shape[1]
    he = jnp.concatenate([hp_ref[0], h_ref[0], hn_ref[0]], axis=0)
    conv_ch = xbc_ref.shape[2]
    raw = _dot(he, w_ref[:, :conv_ch])
    raw = jnp.where(_halo_rows_valid(t, i, n_tiles), raw, 0.0)
    acc = jnp.broadcast_to(cb_ref[...], (t, conv_ch))
    rows = t + 2 * HALO
    for k in range(SSD_CONV):
        back = SSD_CONV // 2 - k
        shifted = raw if back == 0 else pltpu.roll(raw, back % rows, 0)
        acc = acc + cw_ref[k:k + 1, :] * shifted[HALO:HALO + t, :]
    xbc_ref[0] = _silu(acc)
    rest = _dot(h_ref[0], w_ref[:, conv_ch:])
    dt_ref[0] = _softplus(rest[:, :2 * LANES] + dtb_ref[...])
    z_ref[0] = rest[:, 2 * LANES:]


def _ssd_in(h, w, conv_w, conv_b, dt_bias):
    b, l, d = h.shape
    tm = min(ROW_TILE, l)
    n_tiles = l // tm
    conv_ch = conv_w.shape[1]
    const = lambda shp: pl.BlockSpec(shp, lambda bi, i: (0,) * len(shp))
    tok = lambda n: pl.BlockSpec((1, tm, n), lambda bi, i: (bi, i, 0))
    return pl.pallas_call(
        functools.partial(_ssd_in_kernel, n_tiles=n_tiles),
        out_shape=[jax.ShapeDtypeStruct((b, l, conv_ch), F32),
                   jax.ShapeDtypeStruct((b, l, 2 * LANES), F32),
                   jax.ShapeDtypeStruct((b, l, SSD_INNER), F32)],
        grid=(b, n_tiles),
        in_specs=_halo_specs(tm, d, l) + [
            const(w.shape), const(conv_w.shape), const((1, conv_ch)), const((1, 2 * LANES)),
        ],
        out_specs=[tok(conv_ch), tok(2 * LANES), tok(SSD_INNER)],
        name="ssd_in",
    )(h, h, h, w, conv_w, conv_b, dt_bias)


_GROUP_LANES = SSD_INNER // SSD_GROUPS


def _lane_block(t, g, w):
    return t[:, w * g:w * (g + 1)]


def _scan_prepare(chunks, expand):
    n = len(chunks)
    q = chunks[0][0].shape[0]
    gw = _GROUP_LANES
    groups = range(SSD_GROUPS)
    gsl = _lane_block
    c_bf = [[gsl(ch[2], g, SSD_STATE).astype(BF16) for g in groups] for ch in chunks]
    b_t = [[gsl(ch[1], g, SSD_STATE).T.astype(BF16) for g in groups] for ch in chunks]
    cb = [[_dot(c_bf[i][g], b_t[i][g]) for g in groups] for i in range(n)]
    a = [ch[3] * ch[4] for ch in chunks]
    cum = [_dot_split_rhs(ch[5], a[i]) for i, ch in enumerate(chunks)]
    wide = [_dot_split_lhs(jnp.concatenate([cum[i], ch[3]], axis=0), expand)
            for i, ch in enumerate(chunks)]
    rows_tot = cum[0].shape[0] - q
    out = []
    heads_per_group = SSD_HEADS // SSD_GROUPS
    lane_head = lax.broadcasted_iota(jnp.int32, (1, gw), 1) // SSD_HEAD_DIM
    for i, ch in enumerate(chunks):
        xs, mask = ch[0], ch[6]
        acum = cum[i][:q]
        acum_w = wide[i][:q]
        tot_w = wide[i][q:q + 1]
        xd = xs * wide[i][q + rows_tot:]
        xdd = (xd * jnp.exp(tot_w - acum_w)).astype(BF16)
        acum_t = acum.T
        upd = [_dot(b_t[i][g], gsl(xdd, g, gw)) for g in groups]
        y_diag = []
        for g in groups:
            xg = gsl(xd, g, gw)
            lhs, rhs = [], []
            for j in range(heads_per_group):
                head = heads_per_group * g + j
                seg = acum[:, head:head + 1] - acum_t[head:head + 1, :]
                lmat = jnp.exp(jnp.where(mask, seg, -jnp.inf))
                lhs.append((cb[i][g] * lmat).astype(BF16))
                rhs.append(jnp.where(lane_head == j, xg, 0.0).astype(BF16))
            y_diag.append(_dot(jnp.concatenate(lhs, axis=1), jnp.concatenate(rhs, axis=0)))
        out.append((c_bf[i], jnp.concatenate(y_diag, axis=1), jnp.exp(acum_w),
                    jnp.concatenate(upd, axis=1), jnp.exp(tot_w)))
    return out


def _scan_apply(prep, st):
    c_bf, y_diag, decay_out, upd, decay_tot = prep
    st_bf = st.astype(BF16)
    y_off = [_dot(c_bf[g], _lane_block(st_bf, g, _GROUP_LANES)) for g in range(SSD_GROUPS)]
    return y_diag + jnp.concatenate(y_off, axis=1) * decay_out, decay_tot * st + upd


def _ssd_scan_kernel(xf_ref, bf_ref, cf_ref, dtf_ref, xb_ref, bb_ref, cb_ref, dtb_ref,
                     a_ref, tri_ref, mask_ref, e_ref, init_ref, yf_ref, yb_ref, fin_ref, st_ref):
    @pl.when(pl.program_id(1) == 0)
    def _():
        st_ref[...] = init_ref[0]

    q = SSD_CHUNK
    per_step = xf_ref.shape[1] // q
    streams = ((xf_ref, bf_ref, cf_ref, dtf_ref, yf_ref), (xb_ref, bb_ref, cb_ref, dtb_ref, yb_ref))
    order = (list(range(per_step)), list(range(per_step - 1, -1, -1)))
    chunks = []
    for d, (x_ref, b_ref, c_ref, dt_ref, _) in enumerate(streams):
        mask = mask_ref[d] > 0.5
        for j in order[d]:
            rows = pl.ds(j * q, q)
            chunks.append((x_ref[0, rows, :], b_ref[0, rows, :], c_ref[0, rows, :],
                           dt_ref[0, rows, :], a_ref[d], tri_ref[d], mask))
    prep = _scan_prepare(chunks, e_ref[...])
    states = [st_ref[0], st_ref[1]]
    for n in range(per_step):
        for d in range(2):
            y, states[d] = _scan_apply(prep[d * per_step + n], states[d])
            streams[d][4][0, pl.ds(order[d][n] * q, q), :] = y
    for d in range(2):
        st_ref[d] = states[d]
        fin_ref[0, d] = states[d]


def _scan_constants(q):
    r = np.arange(q)
    lower = (r[:, None] >= r[None, :]).astype(np.float32)
    ones = np.ones((SUBLANES, q), np.float32)
    tri = np.stack([np.concatenate([lower, ones], 0), np.concatenate([lower.T, ones], 0)])
    mask = np.stack([lower, lower.T])
    expand = np.zeros((LANES, SSD_INNER), np.float32)
    for hd in range(SSD_HEADS):
        expand[hd, SSD_HEAD_DIM * hd:SSD_HEAD_DIM * (hd + 1)] = 1.0
    return jnp.asarray(tri, BF16), jnp.asarray(mask, F32), jnp.asarray(expand, BF16)


def _ssd_scan(xbc, dt, a_vec, init):
    b, l, _ = xbc.shape
    tri, mask, expand = _scan_constants(SSD_CHUNK)
    q = min(SSD_CHUNK * SCAN_CHUNKS_PER_STEP, l)
    nc = l // q
    bc_w = SSD_GROUPS * SSD_STATE
    bc_blk = SSD_INNER // bc_w
    const = lambda shp: pl.BlockSpec(shp, lambda bi, c: (0,) * len(shp))

    def stream(direction):
        chunk = (lambda c: c) if direction == 0 else (lambda c: nc - 1 - c)
        return [
            pl.BlockSpec((1, q, SSD_INNER), lambda bi, c: (bi, chunk(c), 0)),
            pl.BlockSpec((1, q, bc_w), lambda bi, c: (bi, chunk(c), bc_blk)),
            pl.BlockSpec((1, q, bc_w), lambda bi, c: (bi, chunk(c), bc_blk + 1)),
            pl.BlockSpec((1, q, LANES), lambda bi, c: (bi, chunk(c), direction)),
        ], pl.BlockSpec((1, q, SSD_INNER), lambda bi, c: (bi, chunk(c), 0))

    in_f, out_f = stream(0)
    in_b, out_b = stream(1)
    state_spec = pl.BlockSpec((1, 2, SSD_STATE, SSD_INNER), lambda bi, c: (bi, 0, 0, 0))
    return pl.pallas_call(
        _ssd_scan_kernel,
        out_shape=[jax.ShapeDtypeStruct((b, l, SSD_INNER), F32),
                   jax.ShapeDtypeStruct((b, l, SSD_INNER), F32),
                   jax.ShapeDtypeStruct((b, 2, SSD_STATE, SSD_INNER), F32)],
        grid=(b, nc),
        in_specs=in_f + in_b + [const(a_vec.shape), const(tri.shape), const(mask.shape),
                                const(expand.shape), state_spec],
        out_specs=[out_f, out_b, state_spec],
        scratch_shapes=[pltpu.VMEM((2, SSD_STATE, SSD_INNER), F32)],
        name="ssd_scan",
    )(xbc, xbc, xbc, dt, xbc, xbc, xbc, dt, a_vec, tri, mask, expand, init)


def _conv_residue(v, cw_ref, r, t):
    first = HALO - CM_KERNEL // 2
    part = None
    for al in range(0, 2 * HALO, SUBLANES):
        k = al + r - first
        if 0 <= k < CM_KERNEL:
            term = cw_ref[k:k + 1, :] * v[al:al + t + SUBLANES]
            part = term if part is None else part + term
    return part[r:r + t]


def _conformer_tail(acc, glu_gate, lnw_ref, lnb_ref):
    mu = jnp.mean(acc, axis=-1, keepdims=True)
    dev = acc - mu
    var = jnp.mean(dev * dev, axis=-1, keepdims=True)
    y = _silu(dev * lax.rsqrt(var + EPS) * lnw_ref[...] + lnb_ref[...])
    return y * _silu(glu_gate)


def _merge_kernel(*refs, last, n_tiles):
    (x_ref, hp_ref, h_ref, hn_ref, gt_ref, ua_ref, yf_ref, yb_ref, xs_ref, z_ref, wgm_ref, bg_ref,
     wa_ref, ws_ref, wc_ref, wo_ref, dsk_ref, snw_ref, wcm_ref, cw_ref, cb_ref, lnw_ref, lnb_ref,
     *rest) = refs
    x = x_ref[0]
    t, d = x.shape
    h = h_ref[0]
    he = jnp.concatenate([hp_ref[0], h, hn_ref[0]], axis=0)
    ug = _dot(he, wcm_ref[:, :2 * CM_CH])
    glu_gate = _dot(h, wcm_ref[:, 2 * CM_CH:])
    v = ug[:, :CM_CH] * _sigmoid(ug[:, CM_CH:])
    v = jnp.where(_halo_rows_valid(t, pl.program_id(1), n_tiles), v, 0.0)
    y = yf_ref[0] + yb_ref[0] + dsk_ref[...] * xs_ref[0]
    ua = ua_ref[0].astype(BF16)
    us = _rms(y * _silu(z_ref[0]), snw_ref[...]).astype(BF16)

    def gate_block(j, cols):
        return _sigmoid(_dot(h, wgm_ref[:, d * j + cols.start:d * j + cols.stop])
                        + bg_ref[j:j + 1, cols])

    n_blocks = SUBLANES // 2
    width = d // n_blocks
    conv = jnp.broadcast_to(cb_ref[...], (t, CM_CH))
    partial, conv_gate = [], []
    for r in range(SUBLANES):
        conv = conv + _conv_residue(v, cw_ref, r, t)
        if r % 2 == 1:
            cols = slice((r // 2) * width, (r // 2 + 1) * width)
            partial.append(gate_block(0, cols) * _dot(ua, wa_ref[:, cols])
                           + gate_block(1, cols) * _dot(us, ws_ref[:, cols]))
            conv_gate.append(gate_block(2, cols))
    uc = _conformer_tail(conv, glu_gate, lnw_ref, lnb_ref)
    acc = (jnp.concatenate(partial, axis=1)
           + jnp.concatenate(conv_gate, axis=1) * _dot(uc.astype(BF16), wc_ref[...]))
    out = _dot(acc.astype(BF16), wo_ref[...])
    xn = x + gt_ref[0] * out
    if last:
        fnw_ref, o_ref = rest
        o_ref[0] = _rms(xn, fnw_ref[...])
    else:
        sc_ref, sh_ref, nw_ref, o_ref, hn_ref = rest
        o_ref[0] = xn
        hn_ref[0] = _modulated_norm(xn, nw_ref[...], sc_ref[0], sh_ref[0]).astype(BF16)


def _merge(x, h, gate, ua, yf, yb, xbc, z, wgm, bgate, wa, ws, wc, wo, dskip, snw,
           wcm, cm_conv_w, cm_conv_b, cm_ln_w, cm_ln_b, tail, last):
    b, l, d = x.shape
    tm = min(MERGE_TILE, l)
    const = lambda shp: pl.BlockSpec(shp, lambda bi, i: (0,) * len(shp),
                                     pipeline_mode=pl.Buffered(1))
    tok = lambda n: pl.BlockSpec((1, tm, n), lambda bi, i: (bi, i, 0))
    per_b = pl.BlockSpec((1, 1, d), lambda bi, i: (bi, 0, 0))
    in_specs = [tok(d)] + _halo_specs(tm, d, l) + [
        per_b, tok(ATTN_WIDTH), tok(SSD_INNER), tok(SSD_INNER), tok(SSD_INNER), tok(SSD_INNER),
        const(wgm.shape), const(bgate.shape), const(wa.shape), const(ws.shape),
        const(wc.shape), const(wo.shape), const((1, SSD_INNER)), const((1, SSD_INNER)),
        const(wcm.shape), const(cm_conv_w.shape), const((1, CM_CH)), const((1, CM_CH)),
        const((1, CM_CH)),
    ]
    if last:
        in_specs += [const((1, d))]
        out_shape = jax.ShapeDtypeStruct((b, l, d), F32)
        out_specs = tok(d)
    else:
        in_specs += [per_b, per_b, const((1, d))]
        out_shape = [jax.ShapeDtypeStruct((b, l, d), F32), jax.ShapeDtypeStruct((b, l, d), BF16)]
        out_specs = [tok(d), tok(d)]
    return pl.pallas_call(
        functools.partial(_merge_kernel, last=last, n_tiles=l // tm),
        out_shape=out_shape,
        grid=(b, l // tm),
        in_specs=in_specs,
        out_specs=out_specs,
        name="merge",
    )(x, h, h, h, gate, ua, yf, yb, xbc, z, wgm, bgate, wa, ws, wc, wo, dskip, snw,
      wcm, cm_conv_w, cm_conv_b, cm_ln_w, cm_ln_b, *tail)


def _rope_tables(length):
    rows = length // GRID_W
    row = jnp.repeat(jnp.arange(rows), GRID_W).astype(F32)
    col = jnp.tile(jnp.arange(GRID_W), rows).astype(F32)
    n_freq = HEAD_DIM // 4
    inv = 1.0 / (ROPE_THETA ** (jnp.arange(n_freq, dtype=F32) / n_freq))
    ang_r = row[:, None] * inv
    ang_c = col[:, None] * inv
    zero = jnp.zeros_like(ang_r)
    cos_head = jnp.concatenate([jnp.cos(ang_r)] * 2 + [jnp.cos(ang_c)] * 2, axis=1)
    lo_head = jnp.concatenate([-jnp.sin(ang_r), zero, -jnp.sin(ang_c), zero], axis=1)
    hi_head = jnp.concatenate([zero, jnp.sin(ang_r), zero, jnp.sin(ang_c)], axis=1)
    two = lambda t: jnp.concatenate([t, t], axis=1)
    return two(cos_head), two(lo_head), two(hi_head)


def _head_mean_matrix():
    idx = np.arange(ATTN_WIDTH) // HEAD_DIM
    return jnp.asarray((idx[:, None] == idx[None, :]).astype(np.float32) / HEAD_DIM, BF16)


def _pad_cols(a, width):
    return jnp.pad(a, ((0, 0), (0, width - a.shape[1])))


def kernel(x, c, ctx, c_ctx, w_mod, b_mod, norm_w, w_in, q_norm_w, k_norm_w, ssd_conv_w,
           ssd_conv_b, ssd_A_log, ssd_dt_bias, ssd_D, ssd_norm_w, cm_conv_w, cm_conv_b,
           cm_ln_w, cm_ln_b, w_br_attn, w_br_ssd, w_br_conv, b_gate, w_out, final_norm_w):
    b, l, d = x.shape
    depth = w_mod.shape[0]
    rope_tabs = _rope_tables(l)
    gmat = _head_mean_matrix()

    pad_rows = (-(b + 1)) % SUBLANES
    c_rows = jnp.concatenate([c, c_ctx[None], jnp.zeros((pad_rows, d), F32)], axis=0)
    mod = _modulation(c_rows, w_mod, b_mod)

    def mod_parts(i):
        split = lambda m: (m[:, None, d:2 * d], m[:, None, :d], m[:, None, 2 * d:])
        return split(mod[i, :b]), split(jnp.broadcast_to(mod[i, b:b + 1], (b, 3 * d)))

    o_q, o_k, o_v, o_ga = 0, ATTN_WIDTH, ATTN_WIDTH + KV_WIDTH, ATTN_WIDTH + 2 * KV_WIDTH
    o_xbc = o_ga + ATTN_WIDTH
    conv_ch = SSD_INNER + 2 * SSD_GROUPS * SSD_STATE
    o_dt = o_xbc + conv_ch
    o_z = o_dt + 2 * SSD_HEADS
    o_glu = o_z + SSD_INNER
    o_gcv = o_glu + 2 * CM_CH
    o_gm = o_gcv + CM_CH

    (sc, sh, _), (sc_c, sh_c, _) = mod_parts(0)
    xc = ctx
    h = _hnorm(x, sc, sh, norm_w[0][None])
    hc = _hnorm(xc, sc_c, sh_c, norm_w[0][None])
    for i in range(depth):
        last = i == depth - 1
        wi = w_in[i]
        w_kv = wi[:, o_k:o_ga]
        w_attn = jnp.concatenate([w_kv, wi[:, o_q:o_k], wi[:, o_ga:o_xbc]], axis=1).astype(BF16)
        w_ssd = jnp.concatenate([
            wi[:, o_xbc:o_dt],
            _pad_cols(wi[:, o_dt:o_dt + SSD_HEADS], LANES),
            _pad_cols(wi[:, o_dt + SSD_HEADS:o_z], LANES),
            wi[:, o_z:o_glu]], axis=1).astype(BF16)
        w_cm = wi[:, o_glu:o_gm].astype(BF16)
        w_gm = wi[:, o_gm:].astype(BF16)
        dt_bias = jnp.concatenate([_pad_cols(ssd_dt_bias[i, 0:1], LANES),
                                   _pad_cols(ssd_dt_bias[i, 1:2], LANES)], axis=1)
        a_vec = _pad_cols(-jnp.exp(ssd_A_log[i].astype(F32)), LANES)[:, None, :]
        qnw = jnp.tile(q_norm_w[i], ATTN_HEADS)[None]
        knw = jnp.tile(k_norm_w[i], ATTN_KV_HEADS)[None]
        dskip = jnp.repeat(ssd_D[i], SSD_HEAD_DIM)[None]
        (_, _, gt), (_, _, gt_c) = mod_parts(i)

        if last:
            k_c, vt_c = _attn_in(hc, w_attn[:, :2 * KV_WIDTH], gmat, qnw, knw, None, False)
        else:
            k_c, vt_c, qt_c, ga_c = _attn_in(hc, w_attn, gmat, qnw, knw, None, True)
        k_l, vt_l, qt_l, ga_l = _attn_in(h, w_attn, gmat, qnw, knw, rope_tabs, True)
        score_bound = _score_bound(q_norm_w[i], k_norm_w[i])
        ua = _flash(qt_l, jnp.concatenate([k_l, k_c], axis=1),
                    jnp.concatenate([vt_l, vt_c], axis=3), ga_l, score_bound)

        ssd_args = (w_ssd, ssd_conv_w[i], ssd_conv_b[i][None], dt_bias)
        xbc_c, dt_c, z_c = _ssd_in(hc, *ssd_args)
        zero_state = jnp.zeros((b, 2, SSD_STATE, SSD_INNER), F32)
        yf_c, yb_c, state_c = _ssd_scan(xbc_c, dt_c, a_vec, zero_state)
        xbc_l, dt_l, z_l = _ssd_in(h, *ssd_args)
        yf_l, yb_l, _ = _ssd_scan(xbc_l, dt_l, a_vec, state_c)

        merge_w = (w_gm, b_gate[i], w_br_attn[i].astype(BF16), w_br_ssd[i].astype(BF16),
                   w_br_conv[i].astype(BF16), w_out[i].astype(BF16), dskip, ssd_norm_w[i][None],
                   w_cm, cm_conv_w[i], cm_conv_b[i][None], cm_ln_w[i][None], cm_ln_b[i][None])
        if last:
            x = _merge(x, h, gt, ua, yf_l, yb_l, xbc_l, z_l, *merge_w,
                       (final_norm_w[None],), True)
        else:
            (sc, sh, _), (sc_c, sh_c, _) = mod_parts(i + 1)
            nw_next = norm_w[i + 1][None]
            ua_c = _flash(qt_c, k_c, vt_c, ga_c, score_bound)
            x, h = _merge(x, h, gt, ua, yf_l, yb_l, xbc_l, z_l, *merge_w,
                          (sc, sh, nw_next), False)
            xc, hc = _merge(xc, hc, gt_c, ua_c, yf_c, yb_c, xbc_c, z_c, *merge_w,
                            (sc_c, sh_c, nw_next), False)
    return x
```

```python
import functools

import numpy as np
import jax
import jax.numpy as jnp
from jax import lax
from jax.experimental import pallas as pl
from jax.experimental.pallas import tpu as pltpu

F32 = jnp.float32
BF16 = jnp.bfloat16

GRID_W = 64
HEAD_DIM = 64
ATTN_HEADS = 8
ATTN_KV_HEADS = 2
ATTN_WIDTH = ATTN_HEADS * HEAD_DIM
KV_WIDTH = ATTN_KV_HEADS * HEAD_DIM
ROPE_THETA = 10000.0
ATTN_SCALE = HEAD_DIM ** -0.5
SSD_HEADS = 8
SSD_HEAD_DIM = 64
SSD_INNER = SSD_HEADS * SSD_HEAD_DIM
SSD_GROUPS = 2
SSD_STATE = 128
SSD_CONV = 5
SSD_CHUNK = 128
CM_CH = 512
CM_KERNEL = 31
N_BRANCH = 3
EPS = 1e-6
LOG2_E = 1.4426950408889634
ROPE_PAIR = HEAD_DIM // 4

LANES = 128
SUBLANES = 8
BF16_ROWS = 16
HALO = BF16_ROWS
ROW_TILE = 1024
MERGE_TILE = 512
MOD_COL_TILE = 6 * LANES
Q_TILE = 2048
Q_BLOCK = 512
SCAN_CHUNKS_PER_STEP = 4
MAX_SAFE_SCORE_BOUND = 48.0
MAX_ROWS = 256
V_ROWS = HEAD_DIM + BF16_ROWS


def _dot(a, b):
    return jnp.dot(a, b, preferred_element_type=F32)


def _split2(x):
    x1 = x.astype(BF16)
    return x1, (x - x1.astype(F32)).astype(BF16)


def _split3(x):
    x1 = x.astype(BF16)
    r1 = x - x1.astype(F32)
    x2 = r1.astype(BF16)
    r2 = r1 - x2.astype(F32)
    return x1, x2, r2.astype(BF16)


def _dot_split_lhs(x, m):
    x1, x2 = _split2(x)
    return _dot(x1, m) + _dot(x2, m)


def _dot_split_rhs(m, x):
    x1, x2 = _split2(x)
    return _dot(m, x1) + _dot(m, x2)


def _sigmoid(x):
    return 1.0 / (1.0 + jnp.exp(-x))


def _silu(x):
    return x * _sigmoid(x)


def _softplus(x):
    return jnp.maximum(x, 0.0) + jnp.log1p(jnp.exp(-jnp.abs(x)))


def _modulated_norm(x, nw, scale, shift):
    ms = jnp.mean(x * x, axis=-1, keepdims=True)
    return x * lax.rsqrt(ms + EPS) * (nw * (1.0 + scale)) + shift


def _rms(x, w):
    ms = jnp.mean(x * x, axis=-1, keepdims=True)
    return x * lax.rsqrt(ms + EPS) * w


def _mod_kernel(c_ref, w_ref, b_ref, o_ref):
    a1, a2, a3 = _split3(_silu(c_ref[...]))
    w1, w2, w3 = _split3(w_ref[0])
    o = (_dot(a1, w1) + _dot(a1, w2) + _dot(a2, w1)
         + _dot(a2, w2) + _dot(a1, w3) + _dot(a3, w1))
    o_ref[0] = o + b_ref[0]


def _modulation(c_rows, w_mod, b_mod):
    depth, d, d3 = w_mod.shape
    rows = c_rows.shape[0]
    col_tile = MOD_COL_TILE
    return pl.pallas_call(
        _mod_kernel,
        out_shape=jax.ShapeDtypeStruct((depth, rows, d3), F32),
        grid=(depth, d3 // col_tile),
        in_specs=[
            pl.BlockSpec((rows, d), lambda i, j: (0, 0)),
            pl.BlockSpec((1, d, col_tile), lambda i, j: (i, 0, j)),
            pl.BlockSpec((1, 1, col_tile), lambda i, j: (i, 0, j)),
        ],
        out_specs=pl.BlockSpec((1, rows, col_tile), lambda i, j: (i, 0, j)),
        name="modulation",
    )(c_rows, w_mod, b_mod.reshape(depth, 1, d3))


def _hnorm_kernel(x_ref, sc_ref, sh_ref, nw_ref, h_ref):
    h_ref[0] = _modulated_norm(x_ref[0], nw_ref[...], sc_ref[0], sh_ref[0]).astype(BF16)


def _hnorm(x, scale, shift, nw):
    b, l, d = x.shape
    tm = min(ROW_TILE, l)
    per_b = pl.BlockSpec((1, 1, d), lambda bi, i: (bi, 0, 0))
    tok = pl.BlockSpec((1, tm, d), lambda bi, i: (bi, i, 0))
    return pl.pallas_call(
        _hnorm_kernel,
        out_shape=jax.ShapeDtypeStruct((b, l, d), BF16),
        grid=(b, l // tm),
        in_specs=[tok, per_b, per_b, pl.BlockSpec((1, d), lambda bi, i: (0, 0))],
        out_specs=tok,
        name="hnorm",
    )(x, scale, shift, nw)


def _head_norm(t, gmat, w):
    ms = _dot((t * t).astype(BF16), gmat)
    return t * lax.rsqrt(ms + EPS) * w


def _rope(t, cos, sin_lo, sin_hi):
    outs = []
    for j in range(t.shape[1] // LANES):
        tj = t[:, LANES * j:LANES * (j + 1)]
        outs.append(tj * cos + pltpu.roll(tj, LANES - ROPE_PAIR, 1) * sin_lo
                    + pltpu.roll(tj, ROPE_PAIR, 1) * sin_hi)
    return outs[0] if len(outs) == 1 else jnp.concatenate(outs, axis=1)


def _attn_in_kernel(*refs, rope, need_q):
    if rope:
        h_ref, w_ref, g_ref, qnw_ref, knw_ref, cos_ref, slo_ref, shi_ref, *outs = refs
    else:
        h_ref, w_ref, g_ref, qnw_ref, knw_ref, *outs = refs
    p = _dot(h_ref[0], w_ref[...])
    k = _head_norm(p[:, :KV_WIDTH], g_ref[:KV_WIDTH, :KV_WIDTH], knw_ref[...])
    if rope:
        k = _rope(k, cos_ref[...], slo_ref[...], shi_ref[...])
    if need_q:
        k_ref, vt_ref, qt_ref, ga_ref = outs
    else:
        k_ref, vt_ref = outs
    k_ref[0] = k.astype(BF16)
    vt = p[:, KV_WIDTH:2 * KV_WIDTH].T
    ones = jnp.ones((V_ROWS - HEAD_DIM, vt.shape[1]), F32)
    for g in range(ATTN_KV_HEADS):
        vt_ref[0, g] = jnp.concatenate([vt[HEAD_DIM * g:HEAD_DIM * (g + 1)], ones],
                                       axis=0).astype(BF16)
    if need_q:
        q0 = 2 * KV_WIDTH
        q = _head_norm(p[:, q0:q0 + ATTN_WIDTH], g_ref[...], qnw_ref[...])
        if rope:
            q = _rope(q, cos_ref[...], slo_ref[...], shi_ref[...])
        qt_ref[0] = (q * (ATTN_SCALE * LOG2_E)).T.astype(BF16)
        ga_ref[0] = _silu(p[:, q0 + ATTN_WIDTH:])


def _attn_in(h, w, gmat, qnw, knw, rope_tabs, need_q):
    b, l, d = h.shape
    tm = min(ROW_TILE, l)
    rope = rope_tabs is not None
    const = lambda shp: pl.BlockSpec(shp, lambda bi, i: (0,) * len(shp))
    in_specs = [
        pl.BlockSpec((1, tm, d), lambda bi, i: (bi, i, 0)),
        const(w.shape), const(gmat.shape), const(qnw.shape), const(knw.shape),
    ]
    args = [h, w, gmat, qnw, knw]
    if rope:
        in_specs += [pl.BlockSpec((tm, LANES), lambda bi, i: (i, 0))] * 3
        args += list(rope_tabs)
    tok = lambda n: pl.BlockSpec((1, tm, n), lambda bi, i: (bi, i, 0))
    out_shape = [jax.ShapeDtypeStruct((b, l, KV_WIDTH), BF16),
                 jax.ShapeDtypeStruct((b, ATTN_KV_HEADS, V_ROWS, l), BF16)]
    out_specs = [tok(KV_WIDTH),
                 pl.BlockSpec((1, ATTN_KV_HEADS, V_ROWS, tm), lambda bi, i: (bi, 0, 0, i))]
    if need_q:
        out_shape += [jax.ShapeDtypeStruct((b, ATTN_WIDTH, l), BF16),
                      jax.ShapeDtypeStruct((b, l, ATTN_WIDTH), F32)]
        out_specs += [pl.BlockSpec((1, ATTN_WIDTH, tm), lambda bi, i: (bi, 0, i)),
                      tok(ATTN_WIDTH)]
    return pl.pallas_call(
        functools.partial(_attn_in_kernel, rope=rope, need_q=need_q),
        out_shape=out_shape,
        grid=(b, l // tm),
        in_specs=in_specs,
        out_specs=out_specs,
        name="attn_in",
    )(*args)


def _flash_kernel(*refs, q_block, bounded):
    if bounded:
        bound_ref, qt_ref, k_ref, vt_ref, ga_ref, o_ref = refs
    else:
        qt_ref, k_ref, vt_ref, ga_ref, o_ref = refs
    group = pl.program_id(2) // 2
    k = k_ref[0]
    vt = vt_ref[0, 0]

    def scores(blk, hh):
        qh = qt_ref[0, HEAD_DIM * hh:HEAD_DIM * (hh + 1), pl.ds(blk * q_block, q_block)]
        zero = jnp.zeros_like(qh)
        qpad = jnp.where(group == 0, jnp.concatenate([qh, zero], axis=0),
                         jnp.concatenate([zero, qh], axis=0))
        return _dot(k, qpad)

    def attend(s):
        if bounded:
            shift = bound_ref[...]
        else:
            part = jnp.max(s.reshape(-1, MAX_ROWS, s.shape[1]), axis=0)
            shift = jnp.max(part, axis=0, keepdims=True)
        p = jnp.exp2(s - shift).astype(BF16)
        o = _dot(vt, p)
        return o[:HEAD_DIM] / o[HEAD_DIM:HEAD_DIM + 1]

    chains = [(blk, hh) for blk in range(qt_ref.shape[2] // q_block) for hh in range(2)]
    s_next = scores(*chains[0])
    heads = []
    for n, (blk, hh) in enumerate(chains):
        s_cur = s_next
        if n + 1 < len(chains):
            s_next = scores(*chains[n + 1])
        heads.append(attend(s_cur))
        if hh == 1:
            rows = pl.ds(blk * q_block, q_block)
            o2 = jnp.concatenate(heads, axis=0).T
            o_ref[0, rows, :] = o2 * ga_ref[0, rows, :]
            heads = []


def _flash_call(qt, k, vt, ga, bound):
    b, _, l = qt.shape
    s = k.shape[1]
    tq = min(Q_TILE, l)
    q_block = min(Q_BLOCK, tq)
    in_specs = [
        pl.BlockSpec((1, LANES, tq), lambda bi, i, m: (bi, m, i)),
        pl.BlockSpec((1, s, KV_WIDTH), lambda bi, i, m: (bi, 0, 0)),
        pl.BlockSpec((1, 1, V_ROWS, s), lambda bi, i, m: (bi, m // 2, 0, 0)),
        pl.BlockSpec((1, tq, LANES), lambda bi, i, m: (bi, i, m)),
    ]
    args = [qt, k, vt, ga]
    if bound is not None:
        in_specs.insert(0, pl.BlockSpec((1, q_block), lambda bi, i, m: (0, 0)))
        args.insert(0, jnp.full((1, q_block), bound, F32))
    return pl.pallas_call(
        functools.partial(_flash_kernel, q_block=q_block, bounded=bound is not None),
        out_shape=jax.ShapeDtypeStruct((b, l, ATTN_WIDTH), F32),
        grid=(b, l // tq, ATTN_WIDTH // LANES),
        in_specs=in_specs,
        out_specs=pl.BlockSpec((1, tq, LANES), lambda bi, i, m: (bi, i, m)),
        name="flash_bounded" if bound is not None else "flash",
    )(*args)


def _flash(qt, k, vt, ga, score_bound):
    return lax.cond(score_bound <= MAX_SAFE_SCORE_BOUND,
                    lambda: _flash_call(qt, k, vt, ga, score_bound),
                    lambda: _flash_call(qt, k, vt, ga, None))


def _score_bound(q_norm_w, k_norm_w):
    return (HEAD_DIM * ATTN_SCALE * LOG2_E
            * jnp.max(jnp.abs(q_norm_w)) * jnp.max(jnp.abs(k_norm_w)))


def _halo_rows_valid(tile, i, n_tiles):
    rows = tile + 2 * HALO
    r = lax.broadcasted_iota(jnp.int32, (rows, 1), 0)
    head_ok = jnp.where(i > 0, 0, HALO)
    tail_ok = jnp.where(i < n_tiles - 1, rows, HALO + tile)
    return (r >= head_ok) & (r < tail_ok)


def _halo_specs(tm, d, l):
    per = tm // HALO
    last = l // HALO - 1
    prev = pl.BlockSpec((1, HALO, d), lambda bi, i: (bi, jnp.maximum(i * per - 1, 0), 0))
    cur = pl.BlockSpec((1, tm, d), lambda bi, i: (bi, i, 0))
    nxt = pl.BlockSpec((1, HALO, d), lambda bi, i: (bi, jnp.minimum((i + 1) * per, last), 0))
    return [prev, cur, nxt]


def _ssd_in_kernel(hp_ref, h_ref, hn_ref, w_ref, cw_ref, cb_ref, dtb_ref,
                   xbc_ref, dt_ref, z_ref, *, n_tiles):
    i = pl.program_id(1)
    t = h_ref.shape[1]
    he = jnp.concatenate([hp_ref[0], h_ref[0], hn_ref[0]], axis=0)
    conv_ch = xbc_ref.shape[2]
    raw = _dot(he, w_ref[:, :conv_ch])
    raw = jnp.where(_halo_rows_valid(t, i, n_tiles), raw, 0.0)
    acc = jnp.broadcast_to(cb_ref[...], (t, conv_ch))
    rows = t + 2 * HALO
    for k in range(SSD_CONV):
        back = SSD_CONV // 2 - k
        shifted = raw if back == 0 else pltpu.roll(raw, back % rows, 0)
        acc = acc + cw_ref[k:k + 1, :] * shifted[HALO:HALO + t, :]
    xbc_ref[0] = _silu(acc)
    rest = _dot(h_ref[0], w_ref[:, conv_ch:])
    dt_ref[0] = _softplus(rest[:, :2 * LANES] + dtb_ref[...])
    z_ref[0] = rest[:, 2 * LANES:]


def _ssd_in(h, w, conv_w, conv_b, dt_bias):
    b, l, d = h.shape
    tm = min(ROW_TILE, l)
    n_tiles = l // tm
    conv_ch = conv_w.shape[1]
    const = lambda shp: pl.BlockSpec(shp, lambda bi, i: (0,) * len(shp))
    tok = lambda n: pl.BlockSpec((1, tm, n), lambda bi, i: (bi, i, 0))
    return pl.pallas_call(
        functools.partial(_ssd_in_kernel, n_tiles=n_tiles),
        out_shape=[jax.ShapeDtypeStruct((b, l, conv_ch), F32),
                   jax.ShapeDtypeStruct((b, l, 2 * LANES), F32),
                   jax.ShapeDtypeStruct((b, l, SSD_INNER), F32)],
        grid=(b, n_tiles),
        in_specs=_halo_specs(tm, d, l) + [
            const(w.shape), const(conv_w.shape), const((1, conv_ch)), const((1, 2 * LANES)),
        ],
        out_specs=[tok(conv_ch), tok(2 * LANES), tok(SSD_INNER)],
        name="ssd_in",
    )(h, h, h, w, conv_w, conv_b, dt_bias)


_GROUP_LANES = SSD_INNER // SSD_GROUPS


def _lane_block(t, g, w):
    return t[:, w * g:w * (g + 1)]


def _scan_prepare(chunks, expand):
    n = len(chunks)
    q = chunks[0][0].shape[0]
    gw = _GROUP_LANES
    groups = range(SSD_GROUPS)
    gsl = _lane_block
    c_bf = [[gsl(ch[2], g, SSD_STATE).astype(BF16) for g in groups] for ch in chunks]
    b_t = [[gsl(ch[1], g, SSD_STATE).T.astype(BF16) for g in groups] for ch in chunks]
    cb = [[_dot(c_bf[i][g], b_t[i][g]) for g in groups] for i in range(n)]
    a = [ch[3] * ch[4] for ch in chunks]
    cum = [_dot_split_rhs(ch[5], a[i]) for i, ch in enumerate(chunks)]
    wide = [_dot_split_lhs(jnp.concatenate([cum[i], ch[3]], axis=0), expand)
            for i, ch in enumerate(chunks)]
    rows_tot = cum[0].shape[0] - q
    out = []
    heads_per_group = SSD_HEADS // SSD_GROUPS
    lane_head = lax.broadcasted_iota(jnp.int32, (1, gw), 1) // SSD_HEAD_DIM
    for i, ch in enumerate(chunks):
        xs, mask = ch[0], ch[6]
        acum = cum[i][:q]
        acum_w = wide[i][:q]
        tot_w = wide[i][q:q + 1]
        xd = xs * wide[i][q + rows_tot:]
        xdd = (xd * jnp.exp(tot_w - acum_w)).astype(BF16)
        acum_t = acum.T
        upd = [_dot(b_t[i][g], gsl(xdd, g, gw)) for g in groups]
        y_diag = []
        for g in groups:
            xg = gsl(xd, g, gw)
            lhs, rhs = [], []
            for j in range(heads_per_group):
                head = heads_per_group * g + j
                seg = acum[:, head:head + 1] - acum_t[head:head + 1, :]
                lmat = jnp.exp(jnp.where(mask, seg, -jnp.inf))
                lhs.append((cb[i][g] * lmat).astype(BF16))
                rhs.append(jnp.where(lane_head == j, xg, 0.0).astype(BF16))
            y_diag.append(_dot(jnp.concatenate(lhs, axis=1), jnp.concatenate(rhs, axis=0)))
        out.append((c_bf[i], jnp.concatenate(y_diag, axis=1), jnp.exp(acum_w),
                    jnp.concatenate(upd, axis=1), jnp.exp(tot_w)))
    return out


def _scan_apply(prep, st):
    c_bf, y_diag, decay_out, upd, decay_tot = prep
    st_bf = st.astype(BF16)
    y_off = [_dot(c_bf[g], _lane_block(st_bf, g, _GROUP_LANES)) for g in range(SSD_GROUPS)]
    return y_diag + jnp.concatenate(y_off, axis=1) * decay_out, decay_tot * st + upd


def _ssd_scan_kernel(xf_ref, bf_ref, cf_ref, dtf_ref, xb_ref, bb_ref, cb_ref, dtb_ref,
                     a_ref, tri_ref, mask_ref, e_ref, init_ref, yf_ref, yb_ref, fin_ref, st_ref):
    @pl.when(pl.program_id(1) == 0)
    def _():
        st_ref[...] = init_ref[0]

    q = SSD_CHUNK
    per_step = xf_ref.shape[1] // q
    streams = ((xf_ref, bf_ref, cf_ref, dtf_ref, yf_ref), (xb_ref, bb_ref, cb_ref, dtb_ref, yb_ref))
    order = (list(range(per_step)), list(range(per_step - 1, -1, -1)))
    chunks = []
    for d, (x_ref, b_ref, c_ref, dt_ref, _) in enumerate(streams):
        mask = mask_ref[d] > 0.5
        for j in order[d]:
            rows = pl.ds(j * q, q)
            chunks.append((x_ref[0, rows, :], b_ref[0, rows, :], c_ref[0, rows, :],
                           dt_ref[0, rows, :], a_ref[d], tri_ref[d], mask))
    prep = _scan_prepare(chunks, e_ref[...])
    states = [st_ref[0], st_ref[1]]
    for n in range(per_step):
        for d in range(2):
            y, states[d] = _scan_apply(prep[d * per_step + n], states[d])
            streams[d][4][0, pl.ds(order[d][n] * q, q), :] = y
    for d in range(2):
        st_ref[d] = states[d]
        fin_ref[0, d] = states[d]


def _scan_constants(q):
    r = np.arange(q)
    lower = (r[:, None] >= r[None, :]).astype(np.float32)
    ones = np.ones((SUBLANES, q), np.float32)
    tri = np.stack([np.concatenate([lower, ones], 0), np.concatenate([lower.T, ones], 0)])
    mask = np.stack([lower, lower.T])
    expand = np.zeros((LANES, SSD_INNER), np.float32)
    for hd in range(SSD_HEADS):
        expand[hd, SSD_HEAD_DIM * hd:SSD_HEAD_DIM * (hd + 1)] = 1.0
    return jnp.asarray(tri, BF16), jnp.asarray(mask, F32), jnp.asarray(expand, BF16)


def _ssd_scan(xbc, dt, a_vec, init):
    b, l, _ = xbc.shape
    tri, mask, expand = _scan_constants(SSD_CHUNK)
    q = min(SSD_CHUNK * SCAN_CHUNKS_PER_STEP, l)
    nc = l // q
    bc_w = SSD_GROUPS * SSD_STATE
    bc_blk = SSD_INNER // bc_w
    const = lambda shp: pl.BlockSpec(shp, lambda bi, c: (0,) * len(shp))

    def stream(direction):
        chunk = (lambda c: c) if direction == 0 else (lambda c: nc - 1 - c)
        return [
            pl.BlockSpec((1, q, SSD_INNER), lambda bi, c: (bi, chunk(c), 0)),
            pl.BlockSpec((1, q, bc_w), lambda bi, c: (bi, chunk(c), bc_blk)),
            pl.BlockSpec((1, q, bc_w), lambda bi, c: (bi, chunk(c), bc_blk + 1)),
            pl.BlockSpec((1, q, LANES), lambda bi, c: (bi, chunk(c), direction)),
        ], pl.BlockSpec((1, q, SSD_INNER), lambda bi, c: (bi, chunk(c), 0))

    in_f, out_f = stream(0)
    in_b, out_b = stream(1)
    state_spec = pl.BlockSpec((1, 2, SSD_STATE, SSD_INNER), lambda bi, c: (bi, 0, 0, 0))
    return pl.pallas_call(
        _ssd_scan_kernel,
        out_shape=[jax.ShapeDtypeStruct((b, l, SSD_INNER), F32),
                   jax.ShapeDtypeStruct((b, l, SSD_INNER), F32),
                   jax.ShapeDtypeStruct((b, 2, SSD_STATE, SSD_INNER), F32)],
        grid=(b, nc),
        in_specs=in_f + in_b + [const(a_vec.shape), const(tri.shape), const(mask.shape),
                                const(expand.shape), state_spec],
        out_specs=[out_f, out_b, state_spec],
        scratch_shapes=[pltpu.VMEM((2, SSD_STATE, SSD_INNER), F32)],
        name="ssd_scan",
    )(xbc, xbc, xbc, dt, xbc, xbc, xbc, dt, a_vec, tri, mask, expand, init)


def _conv_residue(v, cw_ref, r, t):
    first = HALO - CM_KERNEL // 2
    part = None
    for al in range(0, 2 * HALO, SUBLANES):
        k = al + r - first
        if 0 <= k < CM_KERNEL:
            term = cw_ref[k:k + 1, :] * v[al:al + t + SUBLANES]
            part = term if part is None else part + term
    return part[r:r + t]


def _conformer_tail(acc, glu_gate, lnw_ref, lnb_ref):
    mu = jnp.mean(acc, axis=-1, keepdims=True)
    dev = acc - mu
    var = jnp.mean(dev * dev, axis=-1, keepdims=True)
    y = _silu(dev * lax.rsqrt(var + EPS) * lnw_ref[...] + lnb_ref[...])
    return y * _silu(glu_gate)


def _merge_kernel(*refs, last, n_tiles):
    (x_ref, hp_ref, h_ref, hn_ref, gt_ref, ua_ref, yf_ref, yb_ref, xs_ref, z_ref, wgm_ref, bg_ref,
     wa_ref, ws_ref, wc_ref, wo_ref, dsk_ref, snw_ref, wcm_ref, cw_ref, cb_ref, lnw_ref, lnb_ref,
     *rest) = refs
    x = x_ref[0]
    t, d = x.shape
    h = h_ref[0]
    he = jnp.concatenate([hp_ref[0], h, hn_ref[0]], axis=0)
    ug = _dot(he, wcm_ref[:, :2 * CM_CH])
    glu_gate = _dot(h, wcm_ref[:, 2 * CM_CH:])
    v = ug[:, :CM_CH] * _sigmoid(ug[:, CM_CH:])
    v = jnp.where(_halo_rows_valid(t, pl.program_id(1), n_tiles), v, 0.0)
    y = yf_ref[0] + yb_ref[0] + dsk_ref[...] * xs_ref[0]
    ua = ua_ref[0].astype(BF16)
    us = _rms(y * _silu(z_ref[0]), snw_ref[...]).astype(BF16)

    def gate_block(j, cols):
        return _sigmoid(_dot(h, wgm_ref[:, d * j + cols.start:d * j + cols.stop])
                        + bg_ref[j:j + 1, cols])

    n_blocks = SUBLANES // 2
    width = d // n_blocks
    conv = jnp.broadcast_to(cb_ref[...], (t, CM_CH))
    partial, conv_gate = [], []
    for r in range(SUBLANES):
        conv = conv + _conv_residue(v, cw_ref, r, t)
        if r % 2 == 1:
            cols = slice((r // 2) * width, (r // 2 + 1) * width)
            partial.append(gate_block(0, cols) * _dot(ua, wa_ref[:, cols])
                           + gate_block(1, cols) * _dot(us, ws_ref[:, cols]))
            conv_gate.append(gate_block(2, cols))
    uc = _conformer_tail(conv, glu_gate, lnw_ref, lnb_ref)
    acc = (jnp.concatenate(partial, axis=1)
           + jnp.concatenate(conv_gate, axis=1) * _dot(uc.astype(BF16), wc_ref[...]))
    out = _dot(acc.astype(BF16), wo_ref[...])
    xn = x + gt_ref[0] * out
    if last:
        fnw_ref, o_ref = rest
        o_ref[0] = _rms(xn, fnw_ref[...])
    else:
        sc_ref, sh_ref, nw_ref, o_ref, hn_ref = rest
        o_ref[0] = xn
        hn_ref[0] = _modulated_norm(xn, nw_ref[...], sc_ref[0], sh_ref[0]).astype(BF16)


def _merge(x, h, gate, ua, yf, yb, xbc, z, wgm, bgate, wa, ws, wc, wo, dskip, snw,
           wcm, cm_conv_w, cm_conv_b, cm_ln_w, cm_ln_b, tail, last):
    b, l, d = x.shape
    tm = min(MERGE_TILE, l)
    const = lambda shp: pl.BlockSpec(shp, lambda bi, i: (0,) * len(shp),
                                     pipeline_mode=pl.Buffered(1))
    tok = lambda n: pl.BlockSpec((1, tm, n), lambda bi, i: (bi, i, 0))
    per_b = pl.BlockSpec((1, 1, d), lambda bi, i: (bi, 0, 0))
    in_specs = [tok(d)] + _halo_specs(tm, d, l) + [
        per_b, tok(ATTN_WIDTH), tok(SSD_INNER), tok(SSD_INNER), tok(SSD_INNER), tok(SSD_INNER),
        const(wgm.shape), const(bgate.shape), const(wa.shape), const(ws.shape),
        const(wc.shape), const(wo.shape), const((1, SSD_INNER)), const((1, SSD_INNER)),
        const(wcm.shape), const(cm_conv_w.shape), const((1, CM_CH)), const((1, CM_CH)),
        const((1, CM_CH)),
    ]
    if last:
        in_specs += [const((1, d))]
        out_shape = jax.ShapeDtypeStruct((b, l, d), F32)
        out_specs = tok(d)
    else:
        in_specs += [per_b, per_b, const((1, d))]
        out_shape = [jax.ShapeDtypeStruct((b, l, d), F32), jax.ShapeDtypeStruct((b, l, d), BF16)]
        out_specs = [tok(d), tok(d)]
    return pl.pallas_call(
        functools.partial(_merge_kernel, last=last, n_tiles=l // tm),
        out_shape=out_shape,
        grid=(b, l // tm),
        in_specs=in_specs,
        out_specs=out_specs,
        name="merge",
    )(x, h, h, h, gate, ua, yf, yb, xbc, z, wgm, bgate, wa, ws, wc, wo, dskip, snw,
      wcm, cm_conv_w, cm_conv_b, cm_ln_w, cm_ln_b, *tail)


def _rope_tables(length):
    rows = length // GRID_W
    row = jnp.repeat(jnp.arange(rows), GRID_W).astype(F32)
    col = jnp.tile(jnp.arange(GRID_W), rows).astype(F32)
    n_freq = HEAD_DIM // 4
    inv = 1.0 / (ROPE_THETA ** (jnp.arange(n_freq, dtype=F32) / n_freq))
    ang_r = row[:, None] * inv
    ang_c = col[:, None] * inv
    zero = jnp.zeros_like(ang_r)
    cos_head = jnp.concatenate([jnp.cos(ang_r)] * 2 + [jnp.cos(ang_c)] * 2, axis=1)
    lo_head = jnp.concatenate([-jnp.sin(ang_r), zero, -jnp.sin(ang_c), zero], axis=1)
    hi_head = jnp.concatenate([zero, jnp.sin(ang_r), zero, jnp.sin(ang_c)], axis=1)
    two = lambda t: jnp.concatenate([t, t], axis=1)
    return two(cos_head), two(lo_head), two(hi_head)


def _head_mean_matrix():
    idx = np.arange(ATTN_WIDTH) // HEAD_DIM
    return jnp.asarray((idx[:, None] == idx[None, :]).astype(np.float32) / HEAD_DIM, BF16)


def _pad_cols(a, width):
    return jnp.pad(a, ((0, 0), (0, width - a.shape[1])))


def kernel(x, c, ctx, c_ctx, w_mod, b_mod, norm_w, w_in, q_norm_w, k_norm_w, ssd_conv_w,
           ssd_conv_b, ssd_A_log, ssd_dt_bias, ssd_D, ssd_norm_w, cm_conv_w, cm_conv_b,
           cm_ln_w, cm_ln_b, w_br_attn, w_br_ssd, w_br_conv, b_gate, w_out, final_norm_w):
    b, l, d = x.shape
    depth = w_mod.shape[0]
    rope_tabs = _rope_tables(l)
    gmat = _head_mean_matrix()

    pad_rows = (-(b + 1)) % SUBLANES
    c_rows = jnp.concatenate([c, c_ctx[None], jnp.zeros((pad_rows, d), F32)], axis=0)
    mod = _modulation(c_rows, w_mod, b_mod)

    def mod_parts(i):
        split = lambda m: (m[:, None, d:2 * d], m[:, None, :d], m[:, None, 2 * d:])
        return split(mod[i, :b]), split(jnp.broadcast_to(mod[i, b:b + 1], (b, 3 * d)))

    o_q, o_k, o_v, o_ga = 0, ATTN_WIDTH, ATTN_WIDTH + KV_WIDTH, ATTN_WIDTH + 2 * KV_WIDTH
    o_xbc = o_ga + ATTN_WIDTH
    conv_ch = SSD_INNER + 2 * SSD_GROUPS * SSD_STATE
    o_dt = o_xbc + conv_ch
    o_z = o_dt + 2 * SSD_HEADS
    o_glu = o_z + SSD_INNER
    o_gcv = o_glu + 2 * CM_CH
    o_gm = o_gcv + CM_CH

    (sc, sh, _), (sc_c, sh_c, _) = mod_parts(0)
    xc = ctx
    h = _hnorm(x, sc, sh, norm_w[0][None])
    hc = _hnorm(xc, sc_c, sh_c, norm_w[0][None])
    for i in range(depth):
        last = i == depth - 1
        wi = w_in[i]
        w_kv = wi[:, o_k:o_ga]
        w_attn = jnp.concatenate([w_kv, wi[:, o_q:o_k], wi[:, o_ga:o_xbc]], axis=1).astype(BF16)
        w_ssd = jnp.concatenate([
            wi[:, o_xbc:o_dt],
            _pad_cols(wi[:, o_dt:o_dt + SSD_HEADS], LANES),
            _pad_cols(wi[:, o_dt + SSD_HEADS:o_z], LANES),
            wi[:, o_z:o_glu]], axis=1).astype(BF16)
        w_cm = wi[:, o_glu:o_gm].astype(BF16)
        w_gm = wi[:, o_gm:].astype(BF16)
        dt_bias = jnp.concatenate([_pad_cols(ssd_dt_bias[i, 0:1], LANES),
                                   _pad_cols(ssd_dt_bias[i, 1:2], LANES)], axis=1)
        a_vec = _pad_cols(-jnp.exp(ssd_A_log[i].astype(F32)), LANES)[:, None, :]
        qnw = jnp.tile(q_norm_w[i], ATTN_HEADS)[None]
        knw = jnp.tile(k_norm_w[i], ATTN_KV_HEADS)[None]
        dskip = jnp.repeat(ssd_D[i], SSD_HEAD_DIM)[None]
        (_, _, gt), (_, _, gt_c) = mod_parts(i)

        if last:
            k_c, vt_c = _attn_in(hc, w_attn[:, :2 * KV_WIDTH], gmat, qnw, knw, None, False)
        else:
            k_c, vt_c, qt_c, ga_c = _attn_in(hc, w_attn, gmat, qnw, knw, None, True)
        k_l, vt_l, qt_l, ga_l = _attn_in(h, w_attn, gmat, qnw, knw, rope_tabs, True)
        score_bound = _score_bound(q_norm_w[i], k_norm_w[i])
        ua = _flash(qt_l, jnp.concatenate([k_l, k_c], axis=1),
                    jnp.concatenate([vt_l, vt_c], axis=3), ga_l, score_bound)

        ssd_args = (w_ssd, ssd_conv_w[i], ssd_conv_b[i][None], dt_bias)
        xbc_c, dt_c, z_c = _ssd_in(hc, *ssd_args)
        zero_state = jnp.zeros((b, 2, SSD_STATE, SSD_INNER), F32)
        yf_c, yb_c, state_c = _ssd_scan(xbc_c, dt_c, a_vec, zero_state)
        xbc_l, dt_l, z_l = _ssd_in(h, *ssd_args)
        yf_l, yb_l, _ = _ssd_scan(xbc_l, dt_l, a_vec, state_c)

        merge_w = (w_gm, b_gate[i], w_br_attn[i].astype(BF16), w_br_ssd[i].astype(BF16),
                   w_br_conv[i].astype(BF16), w_out[i].astype(BF16), dskip, ssd_norm_w[i][None],
                   w_cm, cm_conv_w[i], cm_conv_b[i][None], cm_ln_w[i][None], cm_ln_b[i][None])
        if last:
            x = _merge(x, h, gt, ua, yf_l, yb_l, xbc_l, z_l, *merge_w,
                       (final_norm_w[None],), True)
        else:
            (sc, sh, _), (sc_c, sh_c, _) = mod_parts(i + 1)
            nw_next = norm_w[i + 1][None]
            ua_c = _flash(qt_c, k_c, vt_c, ga_c, score_bound)
            x, h = _merge(x, h, gt, ua, yf_l, yb_l, xbc_l, z_l, *merge_w,
                          (sc, sh, nw_next), False)
            xc, hc = _merge(xc, hc, gt_c, ua_c, yf_c, yb_c, xbc_c, z_c, *merge_w,
                            (sc_c, sh_c, nw_next), False)
    return x
```

```python
import functools

import numpy as np
import jax
import jax.numpy as jnp
from jax import lax
from jax.experimental import pallas as pl
from jax.experimental.pallas import tpu as pltpu

F32 = jnp.float32
BF16 = jnp.bfloat16

GRID_W = 64
HEAD_DIM = 64
ATTN_HEADS = 8
ATTN_KV_HEADS = 2
ATTN_WIDTH = ATTN_HEADS * HEAD_DIM
KV_WIDTH = ATTN_KV_HEADS * HEAD_DIM
ROPE_THETA = 10000.0
ATTN_SCALE = HEAD_DIM ** -0.5
SSD_HEADS = 8
SSD_HEAD_DIM = 64
SSD_INNER = SSD_HEADS * SSD_HEAD_DIM
SSD_GROUPS = 2
SSD_STATE = 128
SSD_CONV = 5
SSD_CHUNK = 128
CM_CH = 512
CM_KERNEL = 31
N_BRANCH = 3
EPS = 1e-6
LOG2_E = 1.4426950408889634
ROPE_PAIR = HEAD_DIM // 4

LANES = 128
SUBLANES = 8
BF16_ROWS = 16
HALO = BF16_ROWS
ROW_TILE = 1024
MERGE_TILE = 512
MOD_COL_TILE = 6 * LANES
Q_TILE = 2048
Q_BLOCK = 512
SCAN_CHUNKS_PER_STEP = 4
MAX_SAFE_SCORE_BOUND = 48.0
MAX_ROWS = 256
V_ROWS = HEAD_DIM + BF16_ROWS


def _dot(a, b):
    return jnp.dot(a, b, preferred_element_type=F32)


def _split2(x):
    x1 = x.astype(BF16)
    return x1, (x - x1.astype(F32)).astype(BF16)


def _split3(x):
    x1 = x.astype(BF16)
    r1 = x - x1.astype(F32)
    x2 = r1.astype(BF16)
    r2 = r1 - x2.astype(F32)
    return x1, x2, r2.astype(BF16)


def _dot_split_lhs(x, m):
    x1, x2 = _split2(x)
    return _dot(x1, m) + _dot(x2, m)


def _dot_split_rhs(m, x):
    x1, x2 = _split2(x)
    return _dot(m, x1) + _dot(m, x2)


def _sigmoid(x):
    return 1.0 / (1.0 + jnp.exp(-x))


def _silu(x):
    return x * _sigmoid(x)


def _softplus(x):
    return jnp.maximum(x, 0.0) + jnp.log1p(jnp.exp(-jnp.abs(x)))


def _modulated_norm(x, nw, scale, shift):
    ms = jnp.mean(x * x, axis=-1, keepdims=True)
    return x * lax.rsqrt(ms + EPS) * (nw * (1.0 + scale)) + shift


def _rms(x, w):
    ms = jnp.mean(x * x, axis=-1, keepdims=True)
    return x * lax.rsqrt(ms + EPS) * w


def _mod_kernel(c_ref, w_ref, b_ref, o_ref):
    a1, a2, a3 = _split3(_silu(c_ref[...]))
    w1, w2, w3 = _split3(w_ref[0])
    o = (_dot(a1, w1) + _dot(a1, w2) + _dot(a2, w1)
         + _dot(a2, w2) + _dot(a1, w3) + _dot(a3, w1))
    o_ref[0] = o + b_ref[0]


def _modulation(c_rows, w_mod, b_mod):
    depth, d, d3 = w_mod.shape
    rows = c_rows.shape[0]
    col_tile = MOD_COL_TILE
    return pl.pallas_call(
        _mod_kernel,
        out_shape=jax.ShapeDtypeStruct((depth, rows, d3), F32),
        grid=(depth, d3 // col_tile),
        in_specs=[
            pl.BlockSpec((rows, d), lambda i, j: (0, 0)),
            pl.BlockSpec((1, d, col_tile), lambda i, j: (i, 0, j)),
            pl.BlockSpec((1, 1, col_tile), lambda i, j: (i, 0, j)),
        ],
        out_specs=pl.BlockSpec((1, rows, col_tile), lambda i, j: (i, 0, j)),
        name="modulation",
    )(c_rows, w_mod, b_mod.reshape(depth, 1, d3))


def _hnorm_kernel(x_ref, sc_ref, sh_ref, nw_ref, h_ref):
    h_ref[0] = _modulated_norm(x_ref[0], nw_ref[...], sc_ref[0], sh_ref[0]).astype(BF16)


def _hnorm(x, scale, shift, nw):
    b, l, d = x.shape
    tm = min(ROW_TILE, l)
    per_b = pl.BlockSpec((1, 1, d), lambda bi, i: (bi, 0, 0))
    tok = pl.BlockSpec((1, tm, d), lambda bi, i: (bi, i, 0))
    return pl.pallas_call(
        _hnorm_kernel,
        out_shape=jax.ShapeDtypeStruct((b, l, d), BF16),
        grid=(b, l // tm),
        in_specs=[tok, per_b, per_b, pl.BlockSpec((1, d), lambda bi, i: (0, 0))],
        out_specs=tok,
        name="hnorm",
    )(x, scale, shift, nw)


def _head_norm(t, gmat, w):
    ms = _dot((t * t).astype(BF16), gmat)
    return t * lax.rsqrt(ms + EPS) * w


def _rope(t, cos, sin_lo, sin_hi):
    outs = []
    for j in range(t.shape[1] // LANES):
        tj = t[:, LANES * j:LANES * (j + 1)]
        outs.append(tj * cos + pltpu.roll(tj, LANES - ROPE_PAIR, 1) * sin_lo
                    + pltpu.roll(tj, ROPE_PAIR, 1) * sin_hi)
    return outs[0] if len(outs) == 1 else jnp.concatenate(outs, axis=1)


def _attn_in_kernel(*refs, rope, need_q):
    if rope:
        h_ref, w_ref, g_ref, qnw_ref, knw_ref, cos_ref, slo_ref, shi_ref, *outs = refs
    else:
        h_ref, w_ref, g_ref, qnw_ref, knw_ref, *outs = refs
    p = _dot(h_ref[0], w_ref[...])
    k = _head_norm(p[:, :KV_WIDTH], g_ref[:KV_WIDTH, :KV_WIDTH], knw_ref[...])
    if rope:
        k = _rope(k, cos_ref[...], slo_ref[...], shi_ref[...])
    if need_q:
        k_ref, vt_ref, qt_ref, ga_ref = outs
    else:
        k_ref, vt_ref = outs
    k_ref[0] = k.astype(BF16)
    vt = p[:, KV_WIDTH:2 * KV_WIDTH].T
    ones = jnp.ones((V_ROWS - HEAD_DIM, vt.shape[1]), F32)
    for g in range(ATTN_KV_HEADS):
        vt_ref[0, g] = jnp.concatenate([vt[HEAD_DIM * g:HEAD_DIM * (g + 1)], ones],
                                       axis=0).astype(BF16)
    if need_q:
        q0 = 2 * KV_WIDTH
        q = _head_norm(p[:, q0:q0 + ATTN_WIDTH], g_ref[...], qnw_ref[...])
        if rope:
            q = _rope(q, cos_ref[...], slo_ref[...], shi_ref[...])
        qt_ref[0] = (q * (ATTN_SCALE * LOG2_E)).T.astype(BF16)
        ga_ref[0] = _silu(p[:, q0 + ATTN_WIDTH:])


def _attn_in(h, w, gmat, qnw, knw, rope_tabs, need_q):
    b, l, d = h.shape
    tm = min(ROW_TILE, l)
    rope = rope_tabs is not None
    const = lambda shp: pl.BlockSpec(shp, lambda bi, i: (0,) * len(shp))
    in_specs = [
        pl.BlockSpec((1, tm, d), lambda bi, i: (bi, i, 0)),
        const(w.shape), const(gmat.shape), const(qnw.shape), const(knw.shape),
    ]
    args = [h, w, gmat, qnw, knw]
    if rope:
        in_specs += [pl.BlockSpec((tm, LANES), lambda bi, i: (i, 0))] * 3
        args += list(rope_tabs)
    tok = lambda n: pl.BlockSpec((1, tm, n), lambda bi, i: (bi, i, 0))
    out_shape = [jax.ShapeDtypeStruct((b, l, KV_WIDTH), BF16),
                 jax.ShapeDtypeStruct((b, ATTN_KV_HEADS, V_ROWS, l), BF16)]
    out_specs = [tok(KV_WIDTH),
                 pl.BlockSpec((1, ATTN_KV_HEADS, V_ROWS, tm), lambda bi, i: (bi, 0, 0, i))]
    if need_q:
        out_shape += [jax.ShapeDtypeStruct((b, ATTN_WIDTH, l), BF16),
                      jax.ShapeDtypeStruct((b, l, ATTN_WIDTH), F32)]
        out_specs += [pl.BlockSpec((1, ATTN_WIDTH, tm), lambda bi, i: (bi, 0, i)),
                      tok(ATTN_WIDTH)]
    return pl.pallas_call(
        functools.partial(_attn_in_kernel, rope=rope, need_q=need_q),
        out_shape=out_shape,
        grid=(b, l // tm),
        in_specs=in_specs,
        out_specs=out_specs,
        name="attn_in",
    )(*args)


def _flash_kernel(*refs, q_block, bounded):
    if bounded:
        bound_ref, *refs = refs
    qt_ref, *kv_refs, ga_ref, o_ref = refs
    n_streams = len(kv_refs) // 2
    group = pl.program_id(2) // 2
    ks = [r[0] for r in kv_refs[:n_streams]]
    vts = [r[0, 0] for r in kv_refs[n_streams:]]

    def scores(blk, hh):
        qh = qt_ref[0, HEAD_DIM * hh:HEAD_DIM * (hh + 1), pl.ds(blk * q_block, q_block)]
        zero = jnp.zeros_like(qh)
        qpad = jnp.where(group == 0, jnp.concatenate([qh, zero], axis=0),
                         jnp.concatenate([zero, qh], axis=0))
        return [_dot(k, qpad) for k in ks]

    def attend(ss):
        if bounded:
            shift = bound_ref[...]
        else:
            tops = [jnp.max(jnp.max(s.reshape(-1, MAX_ROWS, s.shape[1]), axis=0),
                            axis=0, keepdims=True) for s in ss]
            shift = functools.reduce(jnp.maximum, tops)
        o = None
        for s, vt in zip(ss, vts):
            term = _dot(vt, jnp.exp2(s - shift).astype(BF16))
            o = term if o is None else o + term
        return o[:HEAD_DIM] / o[HEAD_DIM:HEAD_DIM + 1]

    chains = [(blk, hh) for blk in range(qt_ref.shape[2] // q_block) for hh in range(2)]
    s_next = scores(*chains[0])
    heads = []
    for n, (blk, hh) in enumerate(chains):
        s_cur = s_next
        if n + 1 < len(chains):
            s_next = scores(*chains[n + 1])
        heads.append(attend(s_cur))
        if hh == 1:
            rows = pl.ds(blk * q_block, q_block)
            o2 = jnp.concatenate(heads, axis=0).T
            o_ref[0, rows, :] = o2 * ga_ref[0, rows, :]
            heads = []


def _flash_call(qt, ks, vts, ga, bound):
    b, _, l = qt.shape
    tq = min(Q_TILE, l)
    q_block = min(Q_BLOCK, tq)
    in_specs = [pl.BlockSpec((1, LANES, tq), lambda bi, i, m: (bi, m, i))]
    in_specs += [pl.BlockSpec((1, k.shape[1], KV_WIDTH), lambda bi, i, m: (bi, 0, 0)) for k in ks]
    in_specs += [pl.BlockSpec((1, 1, V_ROWS, vt.shape[3]), lambda bi, i, m: (bi, m // 2, 0, 0))
                 for vt in vts]
    in_specs += [pl.BlockSpec((1, tq, LANES), lambda bi, i, m: (bi, i, m))]
    args = [qt, *ks, *vts, ga]
    if bound is not None:
        in_specs.insert(0, pl.BlockSpec((1, q_block), lambda bi, i, m: (0, 0)))
        args.insert(0, jnp.full((1, q_block), bound, F32))
    return pl.pallas_call(
        functools.partial(_flash_kernel, q_block=q_block, bounded=bound is not None),
        out_shape=jax.ShapeDtypeStruct((b, l, ATTN_WIDTH), F32),
        grid=(b, l // tq, ATTN_WIDTH // LANES),
        in_specs=in_specs,
        out_specs=pl.BlockSpec((1, tq, LANES), lambda bi, i, m: (bi, i, m)),
        name="flash_bounded" if bound is not None else "flash",
    )(*args)


def _flash(qt, ks, vts, ga, score_bound):
    return lax.cond(score_bound <= MAX_SAFE_SCORE_BOUND,
                    lambda: _flash_call(qt, ks, vts, ga, score_bound),
                    lambda: _flash_call(qt, ks, vts, ga, None))


def _score_bound(q_norm_w, k_norm_w):
    return (HEAD_DIM * ATTN_SCALE * LOG2_E
            * jnp.max(jnp.abs(q_norm_w)) * jnp.max(jnp.abs(k_norm_w)))


def _halo_rows_valid(tile, i, n_tiles):
    rows = tile + 2 * HALO
    r = lax.broadcasted_iota(jnp.int32, (rows, 1), 0)
    head_ok = jnp.where(i > 0, 0, HALO)
    tail_ok = jnp.where(i < n_tiles - 1, rows, HALO + tile)
    return (r >= head_ok) & (r < tail_ok)


def _halo_specs(tm, d, l):
    per = tm // HALO
    last = l // HALO - 1
    prev = pl.BlockSpec((1, HALO, d), lambda bi, i: (bi, jnp.maximum(i * per - 1, 0), 0))
    cur = pl.BlockSpec((1, tm, d), lambda bi, i: (bi, i, 0))
    nxt = pl.BlockSpec((1, HALO, d), lambda bi, i: (bi, jnp.minimum((i + 1) * per, last), 0))
    return [prev, cur, nxt]


def _ssd_in_kernel(hp_ref, h_ref, hn_ref, w_ref, cw_ref, cb_ref, dtb_ref,
                   xbc_ref, dt_ref, z_ref, *, n_tiles):
    i = pl.program_id(1)
    t = h_ref.shape[1]
    he = jnp.concatenate([hp_ref[0], h_ref[0], hn_ref[0]], axis=0)
    conv_ch = xbc_ref.shape[2]
    raw = _dot(he, w_ref[:, :conv_ch])
    raw = jnp.where(_halo_rows_valid(t, i, n_tiles), raw, 0.0)
    acc = jnp.broadcast_to(cb_ref[...], (t, conv_ch))
    rows = t + 2 * HALO
    for k in range(SSD_CONV):
        back = SSD_CONV // 2 - k
        shifted = raw if back == 0 else pltpu.roll(raw, back % rows, 0)
        acc = acc + cw_ref[k:k + 1, :] * shifted[HALO:HALO + t, :]
    xbc_ref[0] = _silu(acc)
    rest = _dot(h_ref[0], w_ref[:, conv_ch:])
    dt_ref[0] = _softplus(rest[:, :2 * LANES] + dtb_ref[...])
    z_ref[0] = rest[:, 2 * LANES:]


def _ssd_in(h, w, conv_w, conv_b, dt_bias):
    b, l, d = h.shape
    tm = min(ROW_TILE, l)
    n_tiles = l // tm
    conv_ch = conv_w.shape[1]
    const = lambda shp: pl.BlockSpec(shp, lambda bi, i: (0,) * len(shp))
    tok = lambda n: pl.BlockSpec((1, tm, n), lambda bi, i: (bi, i, 0))
    return pl.pallas_call(
        functools.partial(_ssd_in_kernel, n_tiles=n_tiles),
        out_shape=[jax.ShapeDtypeStruct((b, l, conv_ch), F32),
                   jax.ShapeDtypeStruct((b, l, 2 * LANES), F32),
                   jax.ShapeDtypeStruct((b, l, SSD_INNER), F32)],
        grid=(b, n_tiles),
        in_specs=_halo_specs(tm, d, l) + [
            const(w.shape), const(conv_w.shape), const((1, conv_ch)), const((1, 2 * LANES)),
        ],
        out_specs=[tok(conv_ch), tok(2 * LANES), tok(SSD_INNER)],
        name="ssd_in",
    )(h, h, h, w, conv_w, conv_b, dt_bias)


_GROUP_LANES = SSD_INNER // SSD_GROUPS


def _lane_block(t, g, w):
    return t[:, w * g:w * (g + 1)]


def _scan_prepare(chunks, expand):
    n = len(chunks)
    q = chunks[0][0].shape[0]
    gw = _GROUP_LANES
    groups = range(SSD_GROUPS)
    gsl = _lane_block
    c_bf = [[gsl(ch[2], g, SSD_STATE).astype(BF16) for g in groups] for ch in chunks]
    b_t = [[gsl(ch[1], g, SSD_STATE).T.astype(BF16) for g in groups] for ch in chunks]
    cb = [[_dot(c_bf[i][g], b_t[i][g]) for g in groups] for i in range(n)]
    a = [ch[3] * ch[4] for ch in chunks]
    cum = [_dot_split_rhs(ch[5], a[i]) for i, ch in enumerate(chunks)]
    wide = [_dot_split_lhs(jnp.concatenate([cum[i], ch[3]], axis=0), expand)
            for i, ch in enumerate(chunks)]
    rows_tot = cum[0].shape[0] - q
    out = []
    heads_per_group = SSD_HEADS // SSD_GROUPS
    lane_head = lax.broadcasted_iota(jnp.int32, (1, gw), 1) // SSD_HEAD_DIM
    for i, ch in enumerate(chunks):
        xs, mask = ch[0], ch[6]
        acum = cum[i][:q]
        acum_w = wide[i][:q]
        tot_w = wide[i][q:q + 1]
        xd = xs * wide[i][q + rows_tot:]
        xdd = (xd * jnp.exp(tot_w - acum_w)).astype(BF16)
        acum_t = acum.T
        upd = [_dot(b_t[i][g], gsl(xdd, g, gw)) for g in groups]
        y_diag = []
        for g in groups:
            xg = gsl(xd, g, gw)
            lhs, rhs = [], []
            for j in range(heads_per_group):
                head = heads_per_group * g + j
                seg = acum[:, head:head + 1] - acum_t[head:head + 1, :]
                lmat = jnp.exp(jnp.where(mask, seg, -jnp.inf))
                lhs.append((cb[i][g] * lmat).astype(BF16))
                rhs.append(jnp.where(lane_head == j, xg, 0.0).astype(BF16))
            y_diag.append(_dot(jnp.concatenate(lhs, axis=1), jnp.concatenate(rhs, axis=0)))
        out.append((c_bf[i], jnp.concatenate(y_diag, axis=1), jnp.exp(acum_w),
                    jnp.concatenate(upd, axis=1), jnp.exp(tot_w)))
    return out


def _scan_apply(prep, st):
    c_bf, y_diag, decay_out, upd, decay_tot = prep
    st_bf = st.astype(BF16)
    y_off = [_dot(c_bf[g], _lane_block(st_bf, g, _GROUP_LANES)) for g in range(SSD_GROUPS)]
    return y_diag + jnp.concatenate(y_off, axis=1) * decay_out, decay_tot * st + upd


def _ssd_scan_kernel(xf_ref, bf_ref, cf_ref, dtf_ref, xb_ref, bb_ref, cb_ref, dtb_ref,
                     a_ref, tri_ref, mask_ref, e_ref, init_ref, yf_ref, yb_ref, fin_ref, st_ref):
    @pl.when(pl.program_id(1) == 0)
    def _():
        st_ref[...] = init_ref[0]

    q = SSD_CHUNK
    per_step = xf_ref.shape[1] // q
    streams = ((xf_ref, bf_ref, cf_ref, dtf_ref, yf_ref), (xb_ref, bb_ref, cb_ref, dtb_ref, yb_ref))
    order = (list(range(per_step)), list(range(per_step - 1, -1, -1)))
    chunks = []
    for d, (x_ref, b_ref, c_ref, dt_ref, _) in enumerate(streams):
        mask = mask_ref[d] > 0.5
        for j in order[d]:
            rows = pl.ds(j * q, q)
            chunks.append((x_ref[0, rows, :], b_ref[0, rows, :], c_ref[0, rows, :],
                           dt_ref[0, rows, :], a_ref[d], tri_ref[d], mask))
    prep = _scan_prepare(chunks, e_ref[...])
    states = [st_ref[0], st_ref[1]]
    for n in range(per_step):
        for d in range(2):
            y, states[d] = _scan_apply(prep[d * per_step + n], states[d])
            streams[d][4][0, pl.ds(order[d][n] * q, q), :] = y
    for d in range(2):
        st_ref[d] = states[d]
        fin_ref[0, d] = states[d]


def _scan_constants(q):
    r = np.arange(q)
    lower = (r[:, None] >= r[None, :]).astype(np.float32)
    ones = np.ones((SUBLANES, q), np.float32)
    tri = np.stack([np.concatenate([lower, ones], 0), np.concatenate([lower.T, ones], 0)])
    mask = np.stack([lower, lower.T])
    expand = np.zeros((LANES, SSD_INNER), np.float32)
    for hd in range(SSD_HEADS):
        expand[hd, SSD_HEAD_DIM * hd:SSD_HEAD_DIM * (hd + 1)] = 1.0
    return jnp.asarray(tri, BF16), jnp.asarray(mask, F32), jnp.asarray(expand, BF16)


def _ssd_scan(xbc, dt, a_vec, init):
    b, l, _ = xbc.shape
    tri, mask, expand = _scan_constants(SSD_CHUNK)
    q = min(SSD_CHUNK * SCAN_CHUNKS_PER_STEP, l)
    nc = l // q
    bc_w = SSD_GROUPS * SSD_STATE
    bc_blk = SSD_INNER // bc_w
    const = lambda shp: pl.BlockSpec(shp, lambda bi, c: (0,) * len(shp))

    def stream(direction):
        chunk = (lambda c: c) if direction == 0 else (lambda c: nc - 1 - c)
        return [
            pl.BlockSpec((1, q, SSD_INNER), lambda bi, c: (bi, chunk(c), 0)),
            pl.BlockSpec((1, q, bc_w), lambda bi, c: (bi, chunk(c), bc_blk)),
            pl.BlockSpec((1, q, bc_w), lambda bi, c: (bi, chunk(c), bc_blk + 1)),
            pl.BlockSpec((1, q, LANES), lambda bi, c: (bi, chunk(c), direction)),
        ], pl.BlockSpec((1, q, SSD_INNER), lambda bi, c: (bi, chunk(c), 0))

    in_f, out_f = stream(0)
    in_b, out_b = stream(1)
    state_spec = pl.BlockSpec((1, 2, SSD_STATE, SSD_INNER), lambda bi, c: (bi, 0, 0, 0))
    return pl.pallas_call(
        _ssd_scan_kernel,
        out_shape=[jax.ShapeDtypeStruct((b, l, SSD_INNER), F32),
                   jax.ShapeDtypeStruct((b, l, SSD_INNER), F32),
                   jax.ShapeDtypeStruct((b, 2, SSD_STATE, SSD_INNER), F32)],
        grid=(b, nc),
        in_specs=in_f + in_b + [const(a_vec.shape), const(tri.shape), const(mask.shape),
                                const(expand.shape), state_spec],
        out_specs=[out_f, out_b, state_spec],
        scratch_shapes=[pltpu.VMEM((2, SSD_STATE, SSD_INNER), F32)],
        name="ssd_scan",
    )(xbc, xbc, xbc, dt, xbc, xbc, xbc, dt, a_vec, tri, mask, expand, init)


def _conv_residue(v, cw_ref, r, t):
    first = HALO - CM_KERNEL // 2
    part = None
    for al in range(0, 2 * HALO, SUBLANES):
        k = al + r - first
        if 0 <= k < CM_KERNEL:
            term = cw_ref[k:k + 1, :] * v[al:al + t + SUBLANES]
            part = term if part is None else part + term
    return part[r:r + t]


def _conformer_tail(acc, glu_gate, lnw_ref, lnb_ref):
    mu = jnp.mean(acc, axis=-1, keepdims=True)
    dev = acc - mu
    var = jnp.mean(dev * dev, axis=-1, keepdims=True)
    y = _silu(dev * lax.rsqrt(var + EPS) * lnw_ref[...] + lnb_ref[...])
    return y * _silu(glu_gate)


def _merge_kernel(*refs, last, n_tiles):
    (x_ref, hp_ref, h_ref, hn_ref, gt_ref, ua_ref, yf_ref, yb_ref, xs_ref, z_ref, wgm_ref, bg_ref,
     wa_ref, ws_ref, wc_ref, wo_ref, dsk_ref, snw_ref, wcm_ref, cw_ref, cb_ref, lnw_ref, lnb_ref,
     *rest) = refs
    x = x_ref[0]
    t, d = x.shape
    h = h_ref[0]
    he = jnp.concatenate([hp_ref[0], h, hn_ref[0]], axis=0)
    ug = _dot(he, wcm_ref[:, :2 * CM_CH])
    glu_gate = _dot(h, wcm_ref[:, 2 * CM_CH:])
    v = ug[:, :CM_CH] * _sigmoid(ug[:, CM_CH:])
    v = jnp.where(_halo_rows_valid(t, pl.program_id(1), n_tiles), v, 0.0)
    y = yf_ref[0] + yb_ref[0] + dsk_ref[...] * xs_ref[0]
    ua = ua_ref[0].astype(BF16)
    us = _rms(y * _silu(z_ref[0]), snw_ref[...]).astype(BF16)

    def gate_block(j, cols):
        return _sigmoid(_dot(h, wgm_ref[:, d * j + cols.start:d * j + cols.stop])
                        + bg_ref[j:j + 1, cols])

    n_blocks = SUBLANES // 2
    width = d // n_blocks
    conv = jnp.broadcast_to(cb_ref[...], (t, CM_CH))
    partial, conv_gate = [], []
    for r in range(SUBLANES):
        conv = conv + _conv_residue(v, cw_ref, r, t)
        if r % 2 == 1:
            cols = slice((r // 2) * width, (r // 2 + 1) * width)
            partial.append(gate_block(0, cols) * _dot(ua, wa_ref[:, cols])
                           + gate_block(1, cols) * _dot(us, ws_ref[:, cols]))
            conv_gate.append(gate_block(2, cols))
    uc = _conformer_tail(conv, glu_gate, lnw_ref, lnb_ref)
    acc = (jnp.concatenate(partial, axis=1)
           + jnp.concatenate(conv_gate, axis=1) * _dot(uc.astype(BF16), wc_ref[...]))
    out = _dot(acc.astype(BF16), wo_ref[...])
    xn = x + gt_ref[0] * out
    if last:
        fnw_ref, o_ref = rest
        o_ref[0] = _rms(xn, fnw_ref[...])
    else:
        sc_ref, sh_ref, nw_ref, o_ref, hn_ref = rest
        o_ref[0] = xn
        hn_ref[0] = _modulated_norm(xn, nw_ref[...], sc_ref[0], sh_ref[0]).astype(BF16)


def _merge(x, h, gate, ua, yf, yb, xbc, z, wgm, bgate, wa, ws, wc, wo, dskip, snw,
           wcm, cm_conv_w, cm_conv_b, cm_ln_w, cm_ln_b, tail, last):
    b, l, d = x.shape
    tm = min(MERGE_TILE, l)
    const = lambda shp: pl.BlockSpec(shp, lambda bi, i: (0,) * len(shp),
                                     pipeline_mode=pl.Buffered(1))
    tok = lambda n: pl.BlockSpec((1, tm, n), lambda bi, i: (bi, i, 0))
    per_b = pl.BlockSpec((1, 1, d), lambda bi, i: (bi, 0, 0))
    in_specs = [tok(d)] + _halo_specs(tm, d, l) + [
        per_b, tok(ATTN_WIDTH), tok(SSD_INNER), tok(SSD_INNER), tok(SSD_INNER), tok(SSD_INNER),
        const(wgm.shape), const(bgate.shape), const(wa.shape), const(ws.shape),
        const(wc.shape), const(wo.shape), const((1, SSD_INNER)), const((1, SSD_INNER)),
        const(wcm.shape), const(cm_conv_w.shape), const((1, CM_CH)), const((1, CM_CH)),
        const((1, CM_CH)),
    ]
    if last:
        in_specs += [const((1, d))]
        out_shape = jax.ShapeDtypeStruct((b, l, d), F32)
        out_specs = tok(d)
    else:
        in_specs += [per_b, per_b, const((1, d))]
        out_shape = [jax.ShapeDtypeStruct((b, l, d), F32), jax.ShapeDtypeStruct((b, l, d), BF16)]
        out_specs = [tok(d), tok(d)]
    return pl.pallas_call(
        functools.partial(_merge_kernel, last=last, n_tiles=l // tm),
        out_shape=out_shape,
        grid=(b, l // tm),
        in_specs=in_specs,
        out_specs=out_specs,
        name="merge",
    )(x, h, h, h, gate, ua, yf, yb, xbc, z, wgm, bgate, wa, ws, wc, wo, dskip, snw,
      wcm, cm_conv_w, cm_conv_b, cm_ln_w, cm_ln_b, *tail)


def _rope_tables(length):
    rows = length // GRID_W
    row = jnp.repeat(jnp.arange(rows), GRID_W).astype(F32)
    col = jnp.tile(jnp.arange(GRID_W), rows).astype(F32)
    n_freq = HEAD_DIM // 4
    inv = 1.0 / (ROPE_THETA ** (jnp.arange(n_freq, dtype=F32) / n_freq))
    ang_r = row[:, None] * inv
    ang_c = col[:, None] * inv
    zero = jnp.zeros_like(ang_r)
    cos_head = jnp.concatenate([jnp.cos(ang_r)] * 2 + [jnp.cos(ang_c)] * 2, axis=1)
    lo_head = jnp.concatenate([-jnp.sin(ang_r), zero, -jnp.sin(ang_c), zero], axis=1)
    hi_head = jnp.concatenate([zero, jnp.sin(ang_r), zero, jnp.sin(ang_c)], axis=1)
    two = lambda t: jnp.concatenate([t, t], axis=1)
    return two(cos_head), two(lo_head), two(hi_head)


def _head_mean_matrix():
    idx = np.arange(ATTN_WIDTH) // HEAD_DIM
    return jnp.asarray((idx[:, None] == idx[None, :]).astype(np.float32) / HEAD_DIM, BF16)


def _pad_cols(a, width):
    return jnp.pad(a, ((0, 0), (0, width - a.shape[1])))


def kernel(x, c, ctx, c_ctx, w_mod, b_mod, norm_w, w_in, q_norm_w, k_norm_w, ssd_conv_w,
           ssd_conv_b, ssd_A_log, ssd_dt_bias, ssd_D, ssd_norm_w, cm_conv_w, cm_conv_b,
           cm_ln_w, cm_ln_b, w_br_attn, w_br_ssd, w_br_conv, b_gate, w_out, final_norm_w):
    b, l, d = x.shape
    depth = w_mod.shape[0]
    rope_tabs = _rope_tables(l)
    gmat = _head_mean_matrix()

    pad_rows = (-(b + 1)) % SUBLANES
    c_rows = jnp.concatenate([c, c_ctx[None], jnp.zeros((pad_rows, d), F32)], axis=0)
    mod = _modulation(c_rows, w_mod, b_mod)

    def mod_parts(i):
        split = lambda m: (m[:, None, d:2 * d], m[:, None, :d], m[:, None, 2 * d:])
        return split(mod[i, :b]), split(jnp.broadcast_to(mod[i, b:b + 1], (b, 3 * d)))

    o_q, o_k, o_v, o_ga = 0, ATTN_WIDTH, ATTN_WIDTH + KV_WIDTH, ATTN_WIDTH + 2 * KV_WIDTH
    o_xbc = o_ga + ATTN_WIDTH
    conv_ch = SSD_INNER + 2 * SSD_GROUPS * SSD_STATE
    o_dt = o_xbc + conv_ch
    o_z = o_dt + 2 * SSD_HEADS
    o_glu = o_z + SSD_INNER
    o_gcv = o_glu + 2 * CM_CH
    o_gm = o_gcv + CM_CH

    (sc, sh, _), (sc_c, sh_c, _) = mod_parts(0)
    xc = ctx
    h = _hnorm(x, sc, sh, norm_w[0][None])
    hc = _hnorm(xc, sc_c, sh_c, norm_w[0][None])
    for i in range(depth):
        last = i == depth - 1
        wi = w_in[i]
        w_kv = wi[:, o_k:o_ga]
        w_attn = jnp.concatenate([w_kv, wi[:, o_q:o_k], wi[:, o_ga:o_xbc]], axis=1).astype(BF16)
        w_ssd = jnp.concatenate([
            wi[:, o_xbc:o_dt],
            _pad_cols(wi[:, o_dt:o_dt + SSD_HEADS], LANES),
            _pad_cols(wi[:, o_dt + SSD_HEADS:o_z], LANES),
            wi[:, o_z:o_glu]], axis=1).astype(BF16)
        w_cm = wi[:, o_glu:o_gm].astype(BF16)
        w_gm = wi[:, o_gm:].astype(BF16)
        dt_bias = jnp.concatenate([_pad_cols(ssd_dt_bias[i, 0:1], LANES),
                                   _pad_cols(ssd_dt_bias[i, 1:2], LANES)], axis=1)
        a_vec = _pad_cols(-jnp.exp(ssd_A_log[i].astype(F32)), LANES)[:, None, :]
        qnw = jnp.tile(q_norm_w[i], ATTN_HEADS)[None]
        knw = jnp.tile(k_norm_w[i], ATTN_KV_HEADS)[None]
        dskip = jnp.repeat(ssd_D[i], SSD_HEAD_DIM)[None]
        (_, _, gt), (_, _, gt_c) = mod_parts(i)

        if last:
            k_c, vt_c = _attn_in(hc, w_attn[:, :2 * KV_WIDTH], gmat, qnw, knw, None, False)
        else:
            k_c, vt_c, qt_c, ga_c = _attn_in(hc, w_attn, gmat, qnw, knw, None, True)
        k_l, vt_l, qt_l, ga_l = _attn_in(h, w_attn, gmat, qnw, knw, rope_tabs, True)
        score_bound = _score_bound(q_norm_w[i], k_norm_w[i])
        ua = _flash(qt_l, (k_l, k_c), (vt_l, vt_c), ga_l, score_bound)

        ssd_args = (w_ssd, ssd_conv_w[i], ssd_conv_b[i][None], dt_bias)
        xbc_c, dt_c, z_c = _ssd_in(hc, *ssd_args)
        zero_state = jnp.zeros((b, 2, SSD_STATE, SSD_INNER), F32)
        yf_c, yb_c, state_c = _ssd_scan(xbc_c, dt_c, a_vec, zero_state)
        xbc_l, dt_l, z_l = _ssd_in(h, *ssd_args)
        yf_l, yb_l, _ = _ssd_scan(xbc_l, dt_l, a_vec, state_c)

        merge_w = (w_gm, b_gate[i], w_br_attn[i].astype(BF16), w_br_ssd[i].astype(BF16),
                   w_br_conv[i].astype(BF16), w_out[i].astype(BF16), dskip, ssd_norm_w[i][None],
                   w_cm, cm_conv_w[i], cm_conv_b[i][None], cm_ln_w[i][None], cm_ln_b[i][None])
        if last:
            x = _merge(x, h, gt, ua, yf_l, yb_l, xbc_l, z_l, *merge_w,
                       (final_norm_w[None],), True)
        else:
            (sc, sh, _), (sc_c, sh_c, _) = mod_parts(i + 1)
            nw_next = norm_w[i + 1][None]
            ua_c = _flash(qt_c, (k_c,), (vt_c,), ga_c, score_bound)
            x, h = _merge(x, h, gt, ua, yf_l, yb_l, xbc_l, z_l, *merge_w,
                          (sc, sh, nw_next), False)
            xc, hc = _merge(xc, hc, gt_c, ua_c, yf_c, yb_c, xbc_c, z_c, *merge_w,
                            (sc_c, sh_c, nw_next), False)
    return x
```
